```python
import math
import jax, jax.numpy as jnp
from jax import lax
import numpy as np


D_MODEL = 2048
BATCH = 8
SEQ = 2048
DEPTH = 2
DEC_BATCH = 128
DEC_SEQ = 1
PAST_LEN = 2048
PAGE_SIZE = 128

MIX_WIDTH = D_MODEL
HEAD_DIM = 128
A_WIDTH = MIX_WIDTH // 2
A_HEADS = A_WIDTH // HEAD_DIM
B_WIDTH = MIX_WIDTH - A_WIDTH
CONV_B_WIDTH = 31
C_WIDTH = MIX_WIDTH
CONV_C_WIDTH = 3
N_EXPERTS = 32
TOP_K = 4
D_FF = D_MODEL
SWIGLU_LIMIT = 7.0
SWIGLU_ALPHA = 1.702
Q_BLOCK = 128
MOE_BLOCK = 128
NORM_EPS = 1e-6
SB_BIAS_INIT = -5.0
N_A_LAYERS = (DEPTH + 1) // 2
N_C_LAYERS = DEPTH // 2

kernel_name = 'hybrid_stickbreak_conformer_shortconv_moe_step'


def rms_norm(x, g):
    xf = x.astype(jnp.float32)
    y = xf * lax.rsqrt(jnp.mean(xf * xf, axis=-1, keepdims=True) + NORM_EPS)
    return (y * g.astype(jnp.float32)).astype(x.dtype)


def layer_norm(x, g, b):
    xf = x.astype(jnp.float32)
    mu = jnp.mean(xf, axis=-1, keepdims=True)
    xc = xf - mu
    var = jnp.mean(xc * xc, axis=-1, keepdims=True)
    return (xc * lax.rsqrt(var + NORM_EPS) * g.astype(jnp.float32) + b.astype(jnp.float32)).astype(x.dtype)


def ada_modulate(x, c, g, w_ada, b_ada):
    m = jax.nn.silu(c) @ w_ada + b_ada
    shift, scale, gate = jnp.split(m, 3, axis=-1)
    h = rms_norm(x, g) * (1 + scale[:, None, :]) + shift[:, None, :]
    return h, gate[:, None, :]


def causal_depthwise_conv(hist, u, w):
    full = jnp.concatenate([hist, u.astype(hist.dtype)], axis=1)
    out = lax.conv_general_dilated(full, w[:, None, :].astype(full.dtype), (1,), 'VALID',
                                   dimension_numbers=('NWC', 'WIO', 'NWC'),
                                   feature_group_count=u.shape[-1])
    return out, full[:, full.shape[1] - (w.shape[0] - 1):]


def stick_breaking_attention(q, k, v, q_offset, sb_bias):
    b, tq, h, d = q.shape
    tk = k.shape[1]
    qb = Q_BLOCK if tq % Q_BLOCK == 0 else tq
    nb = tq // qb
    q_blocks = q.reshape(b, nb, qb, h, d).transpose(1, 0, 2, 3, 4)
    key_pos = jnp.arange(tk)
    scale = 1.0 / math.sqrt(d)
    bias = sb_bias.astype(jnp.float32)[None, :, None, None]

    def one_block(args):
        q_blk, bi = args
        q_pos = q_offset + bi * qb + jnp.arange(qb)
        z = jnp.einsum('bqhd,bkhd->bhqk', q_blk, k).astype(jnp.float32) * scale + bias
        mask = key_pos[None, :] < q_pos[:, None]
        log_keep = jnp.where(mask, jax.nn.log_sigmoid(-z), 0.0)
        log_after = lax.cumsum(log_keep, axis=3, reverse=True) - log_keep
        a = jnp.where(mask, jnp.exp(jax.nn.log_sigmoid(z) + log_after), 0.0)
        return jnp.einsum('bhqk,bkhd->bqhd', a.astype(v.dtype), v)

    o = lax.map(one_block, (q_blocks, jnp.arange(nb)))
    return o.transpose(1, 0, 2, 3, 4).reshape(b, tq, h, d)


def even_mixer(h, k_past, v_past, hist_b, w_in, g_q, g_k, sb_bias, conv_w, conv_bias, ln_g, ln_b, w_out):
    b, t, _ = h.shape
    proj = h @ w_in
    q, k, v, u_val, u_gate = jnp.split(
        proj, [A_WIDTH, 2 * A_WIDTH, 3 * A_WIDTH, 3 * A_WIDTH + B_WIDTH], axis=-1)
    q = rms_norm(q.reshape(b, t, A_HEADS, HEAD_DIM), g_q)
    k = rms_norm(k.reshape(b, t, A_HEADS, HEAD_DIM), g_k)
    v = v.reshape(b, t, A_HEADS, HEAD_DIM)
    if k_past is None:
        k_all, v_all, past = k, v, 0
    else:
        k_all = jnp.concatenate([k_past.astype(k.dtype), k], axis=1)
        v_all = jnp.concatenate([v_past.astype(v.dtype), v], axis=1)
        past = k_past.shape[1]
    o_a = stick_breaking_attention(q, k_all, v_all, past, sb_bias)
    u = u_val * jax.nn.sigmoid(u_gate)
    conv, new_hist = causal_depthwise_conv(hist_b, u, conv_w)
    y_b = jax.nn.silu(layer_norm(conv + conv_bias, ln_g, ln_b))
    out = jnp.concatenate([o_a.reshape(b, t, A_WIDTH), y_b.astype(o_a.dtype)], axis=-1) @ w_out
    return out, k, v, new_hist


def odd_mixer(h, hist_c, w_in, conv_w, w_out):
    gate_b, gate_c, hv = jnp.split(h @ w_in, 3, axis=-1)
    conv, new_hist = causal_depthwise_conv(hist_c, gate_c * hv, conv_w)
    return (gate_b * conv) @ w_out, new_hist


def clamped_swiglu(gu):
    gate, up = jnp.split(gu, 2, axis=-1)
    gate = jnp.minimum(gate, SWIGLU_LIMIT)
    up = jnp.clip(up, -SWIGLU_LIMIT, SWIGLU_LIMIT)
    return (up + 1) * gate * jax.nn.sigmoid(SWIGLU_ALPHA * gate)


def moe_ffn(h, w_router, b_router, w_gu, b_gu, w_dn, b_dn):
    bsz, t, d = h.shape
    n_tok = bsz * t
    xt = h.reshape(n_tok, d)
    logits = (xt @ w_router).astype(jnp.float32) + b_router.astype(jnp.float32)
    top_val, top_idx = lax.top_k(logits, TOP_K)
    gates = jax.nn.softmax(top_val, axis=-1)
    n = n_tok * TOP_K
    flat_e = top_idx.reshape(n)
    flat_tok = jnp.arange(n, dtype=jnp.int32) // TOP_K
    flat_g = gates.reshape(n)
    order = jnp.argsort(flat_e)
    sorted_e = flat_e[order]
    counts = jnp.bincount(flat_e, length=N_EXPERTS)
    starts = jnp.cumsum(counts) - counts
    padded = (counts + MOE_BLOCK - 1) // MOE_BLOCK * MOE_BLOCK
    pad_ends = jnp.cumsum(padded)
    pad_starts = pad_ends - padded
    dest = pad_starts[sorted_e] + jnp.arange(n) - starts[sorted_e]
    n_blocks = (n + N_EXPERTS * (MOE_BLOCK - 1)) // MOE_BLOCK
    n_rows = n_blocks * MOE_BLOCK
    row_tok = jnp.zeros((n_rows,), jnp.int32).at[dest].set(flat_tok[order])
    row_g = jnp.zeros((n_rows,), jnp.float32).at[dest].set(flat_g[order])
    blk_e = jnp.minimum(jnp.searchsorted(pad_ends, jnp.arange(n_blocks) * MOE_BLOCK, side='right'),
                        N_EXPERTS - 1)
    xs = xt[row_tok].reshape(n_blocks, MOE_BLOCK, d)

    def expert_block(args):
        xb, e = args
        act = clamped_swiglu(xb @ w_gu[e] + b_gu[e])
        return act @ w_dn[e] + b_dn[e]

    ys = lax.map(expert_block, (xs, blk_e)).reshape(n_rows, d)
    y = jax.ops.segment_sum(ys * row_g[:, None].astype(ys.dtype), row_tok, num_segments=n_tok)
    return y.reshape(bsz, t, d)


def trunk(x, c, cache_k, cache_v, page_table, hists_b, hists_c, p):
    new_k, new_v, new_b, new_c = [], [], [], []
    for i in range(DEPTH):
        j = i // 2
        h, gate = ada_modulate(x, c, p['norm_g'][i, 0], p['w_ada'][i, 0], p['b_ada'][i, 0])
        if i % 2 == 0:
            if cache_k is None:
                k_past, v_past = None, None
            else:
                dbs, n_pages = page_table.shape
                past = n_pages * cache_k.shape[2]
                k_past = cache_k[j][page_table].reshape(dbs, past, A_HEADS, HEAD_DIM)
                v_past = cache_v[j][page_table].reshape(dbs, past, A_HEADS, HEAD_DIM)
            out, k, v, hb = even_mixer(h, k_past, v_past, hists_b[j], p['w_in_a'][j], p['g_q'][j], p['g_k'][j],
                                       p['sb_bias'][j], p['conv_b_w'][j], p['conv_b_bias'][j], p['ln_b_g'][j],
                                       p['ln_b_b'][j], p['w_out_a'][j])
            new_k.append(k)
            new_v.append(v)
            new_b.append(hb)
        else:
            out, hc = odd_mixer(h, hists_c[j], p['w_in_c'][j], p['conv_c_w'][j], p['w_out_c'][j])
            new_c.append(hc)
        x = x + gate * out
        h, gate = ada_modulate(x, c, p['norm_g'][i, 1], p['w_ada'][i, 1], p['b_ada'][i, 1])
        x = x + gate * moe_ffn(h, p['w_router'][i], p['b_router'][i], p['w_gate_up'][i], p['b_gate_up'][i],
                               p['w_down'][i], p['b_down'][i])
    return x, jnp.stack(new_k), jnp.stack(new_v), jnp.stack(new_b), jnp.stack(new_c)


def setup_inputs(seed: int = 0) -> dict:
    key = jax.random.key(seed)
    ks = jax.random.split(key, 32)
    f32 = jnp.float32

    def nrm(k, shape, s):
        return jax.random.normal(k, shape, f32) * s

    n_pages = PAST_LEN // PAGE_SIZE
    n_used = DEC_BATCH * n_pages
    n_phys = n_used + max(1, n_used // 4)
    page_table = jax.random.permutation(ks[8], n_phys)[:n_used].reshape(DEC_BATCH, n_pages).astype(jnp.int32)
    d = D_MODEL
    return {
        'x_prompt': nrm(ks[0], (BATCH, SEQ, d), 1.0),
        'x_sample': nrm(ks[1], (DEC_BATCH, DEC_SEQ, d), 1.0),
        'c_prompt': nrm(ks[2], (BATCH, d), 1.0),
        'c_sample': nrm(ks[3], (DEC_BATCH, d), 1.0),
        'cache_k': nrm(ks[4], (N_A_LAYERS, n_phys, PAGE_SIZE, A_HEADS, HEAD_DIM), 1.0),
        'cache_v': nrm(ks[5], (N_A_LAYERS, n_phys, PAGE_SIZE, A_HEADS, HEAD_DIM), 1.0),
        'state_conv_b': nrm(ks[6], (N_A_LAYERS, DEC_BATCH, CONV_B_WIDTH - 1, B_WIDTH), 1.0),
        'state_conv_c': nrm(ks[7], (N_C_LAYERS, DEC_BATCH, CONV_C_WIDTH - 1, C_WIDTH), 1.0),
        'page_table': page_table,
        'norm_g': 1.0 + nrm(ks[9], (DEPTH, 2, d), 0.05),
        'w_ada': nrm(ks[10], (DEPTH, 2, d, 3 * d), 0.5 * d ** -0.5),
        'b_ada': nrm(ks[11], (DEPTH, 2, 3 * d), 0.02),
        'w_in_a': nrm(ks[12], (N_A_LAYERS, d, 3 * A_WIDTH + 2 * B_WIDTH), d ** -0.5),
        'g_q': 1.0 + nrm(ks[13], (N_A_LAYERS, HEAD_DIM), 0.05),
        'g_k': 1.0 + nrm(ks[14], (N_A_LAYERS, HEAD_DIM), 0.05),
        'sb_bias': SB_BIAS_INIT + nrm(ks[29], (N_A_LAYERS, A_HEADS), 0.1),
        'conv_b_w': nrm(ks[15], (N_A_LAYERS, CONV_B_WIDTH, B_WIDTH), CONV_B_WIDTH ** -0.5),
        'conv_b_bias': nrm(ks[16], (N_A_LAYERS, B_WIDTH), 0.02),
        'ln_b_g': 1.0 + nrm(ks[17], (N_A_LAYERS, B_WIDTH), 0.05),
        'ln_b_b': nrm(ks[18], (N_A_LAYERS, B_WIDTH), 0.02),
        'w_out_a': nrm(ks[19], (N_A_LAYERS, MIX_WIDTH, d), MIX_WIDTH ** -0.5),
        'w_in_c': nrm(ks[20], (N_C_LAYERS, d, 3 * C_WIDTH), d ** -0.5),
        'conv_c_w': nrm(ks[21], (N_C_LAYERS, CONV_C_WIDTH, C_WIDTH), CONV_C_WIDTH ** -0.5),
        'w_out_c': nrm(ks[22], (N_C_LAYERS, C_WIDTH, d), C_WIDTH ** -0.5),
        'w_router': nrm(ks[23], (DEPTH, d, N_EXPERTS), d ** -0.5),
        'b_router': nrm(ks[24], (DEPTH, N_EXPERTS), 0.01),
        'w_gate_up': nrm(ks[25], (DEPTH, N_EXPERTS, d, 2 * D_FF), d ** -0.5),
        'b_gate_up': nrm(ks[26], (DEPTH, N_EXPERTS, 2 * D_FF), 0.01),
        'w_down': nrm(ks[27], (DEPTH, N_EXPERTS, D_FF, d), D_FF ** -0.5),
        'b_down': nrm(ks[28], (DEPTH, N_EXPERTS, d), 0.01),
    }


def reference(x_prompt, x_sample, c_prompt, c_sample, cache_k, cache_v, state_conv_b, state_conv_c, page_table,
              norm_g, w_ada, b_ada, w_in_a, g_q, g_k, sb_bias, conv_b_w, conv_b_bias, ln_b_g, ln_b_b, w_out_a,
              w_in_c, conv_c_w, w_out_c, w_router, b_router, w_gate_up, b_gate_up, w_down, b_down):
    p = {'norm_g': norm_g, 'w_ada': w_ada, 'b_ada': b_ada, 'w_in_a': w_in_a, 'g_q': g_q, 'g_k': g_k,
         'sb_bias': sb_bias, 'conv_b_w': conv_b_w, 'conv_b_bias': conv_b_bias, 'ln_b_g': ln_b_g,
         'ln_b_b': ln_b_b, 'w_out_a': w_out_a, 'w_in_c': w_in_c, 'conv_c_w': conv_c_w, 'w_out_c': w_out_c,
         'w_router': w_router, 'b_router': b_router, 'w_gate_up': w_gate_up, 'b_gate_up': b_gate_up,
         'w_down': w_down, 'b_down': b_down}
    zero_b = jnp.zeros((N_A_LAYERS, x_prompt.shape[0], CONV_B_WIDTH - 1, B_WIDTH), x_prompt.dtype)
    zero_c = jnp.zeros((N_C_LAYERS, x_prompt.shape[0], CONV_C_WIDTH - 1, C_WIDTH), x_prompt.dtype)
    y_prompt, k_prompt, v_prompt, conv_b_prompt, conv_c_prompt = trunk(
        x_prompt, c_prompt, None, None, None, zero_b, zero_c, p)
    y_sample, k_sample, v_sample, conv_b_sample, conv_c_sample = trunk(
        x_sample, c_sample, cache_k, cache_v, page_table, state_conv_b, state_conv_c, p)
    return (y_prompt, y_sample, k_prompt, v_prompt, conv_b_prompt, conv_c_prompt,
            k_sample, v_sample, conv_b_sample, conv_c_sample)
```

```python
import functools
import math

import jax
import jax.numpy as jnp
from jax import lax
from jax.experimental import pallas as pl
from jax.experimental.pallas import tpu as pltpu

F32 = jnp.float32
BF16 = jnp.bfloat16
I32 = jnp.int32

LANES = 128
SUBLANES = 8
VMEM_LIMIT_BYTES = 56 * 1024 * 1024

HEAD_DIM = 128
TOP_K = 4
N_EXPERTS = 32
SWIGLU_LIMIT = 7.0
SWIGLU_ALPHA = 1.702
NORM_EPS = 1e-6

ROW_TILE = 128
MOE_BLOCK = 512
MOE_FF_TILE = 256
PROJ_TM = 512
PROJ_TN = 512
ATTN_BQ = 256
ATTN_BK = 256
CONV_TT = 256
CONV_HALO = 32
SHORT_HALO = 16
DECODE_PAGES_PER_STEP = 4


def _cparams(sem):
    return pltpu.CompilerParams(dimension_semantics=sem, vmem_limit_bytes=VMEM_LIMIT_BYTES)


def _silu(x):
    return x * jax.nn.sigmoid(x)


def _ada_kernel(c_ref, w_ref, b_ref, o_ref):
    c = c_ref[...]
    s = _silu(c).astype(BF16)
    o_ref[...] = jnp.dot(s, w_ref[...].astype(BF16), preferred_element_type=F32) + b_ref[...]


def _ada_all(c_all, w_ada, b_ada):
    nseq, d = c_all.shape
    nslot = w_ada.shape[0] * w_ada.shape[1]
    n3 = w_ada.shape[-1]
    w4 = w_ada.reshape(nslot, d, n3)
    b4 = b_ada.reshape(nslot, 1, n3)
    tn = 1024
    return pl.pallas_call(
        _ada_kernel,
        grid=(nslot, n3 // tn),
        in_specs=[
            pl.BlockSpec((nseq, d), lambda s, j: (0, 0)),
            pl.BlockSpec((None, d, tn), lambda s, j: (s, 0, j)),
            pl.BlockSpec((None, 1, tn), lambda s, j: (s, 0, j)),
        ],
        out_specs=pl.BlockSpec((None, nseq, tn), lambda s, j: (s, 0, j)),
        out_shape=jax.ShapeDtypeStruct((nslot, nseq, n3), F32),
        compiler_params=_cparams(("arbitrary", "arbitrary")),
        name="ada_mod",
    )(c_all, w4, b4)


class _Mod:
    def __init__(self, arr, per_row, rows_per_seq, d):
        self.arr = arr
        self.per_row = per_row
        self.rows_per_seq = rows_per_seq
        self.d = d

    def spec(self, slot, part, tm, tn, row_axis, col_axis=None, grid_rank=1):
        ncol = self.d // tn

        def col(idx):
            return part * ncol + (idx[col_axis] if col_axis is not None else 0)

        if self.per_row:
            return pl.BlockSpec((None, tm, tn), lambda *idx: (slot, idx[row_axis], col(idx)))
        tps = self.rows_per_seq // tm
        return pl.BlockSpec((None, None, 1, tn), lambda *idx: (slot, idx[row_axis] // tps, 0, col(idx)))


def _normed(x, g, shift, scale):
    y = x * lax.rsqrt(jnp.mean(x * x, axis=-1, keepdims=True) + NORM_EPS)
    return y * g * (1.0 + scale) + shift


def _norm_mod_kernel(x_ref, g_ref, shift_ref, scale_ref, h_ref):
    h_ref[...] = _normed(x_ref[...], g_ref[...], shift_ref[...], scale_ref[...]).astype(h_ref.dtype)


def _norm_mod(x, g4, mod, slot, tm):
    m, d = x.shape
    return pl.pallas_call(
        _norm_mod_kernel,
        grid=(m // tm,),
        in_specs=[
            pl.BlockSpec((tm, d), lambda i: (i, 0)),
            pl.BlockSpec((None, 1, d), lambda i: (slot, 0, 0)),
            mod.spec(slot, 0, tm, d, 0),
            mod.spec(slot, 1, tm, d, 0),
        ],
        out_specs=pl.BlockSpec((tm, d), lambda i: (i, 0)),
        out_shape=jax.ShapeDtypeStruct((m, d), BF16),
        compiler_params=_cparams(("arbitrary",)),
        name="norm_mod",
    )(x, g4, mod.arr, mod.arr)


def _split3(x):
    hi = x.astype(BF16)
    r = x - hi.astype(F32)
    mid = r.astype(BF16)
    lo = (r - mid.astype(F32)).astype(BF16)
    return hi, mid, lo


def _dot_f32(a, b):
    a0, a1, a2 = _split3(a)
    b0, b1, b2 = _split3(b)
    dot = functools.partial(jnp.dot, preferred_element_type=F32)
    small = dot(a0, b2) + dot(a2, b0) + dot(a1, b1)
    mid = dot(a0, b1) + dot(a1, b0)
    return dot(a0, b0) + (mid + small)


def _lane_place(cols, shape, dtype):
    lane = lax.broadcasted_iota(I32, shape, 1)
    out = jnp.zeros(shape, dtype)
    for k, c in enumerate(cols):
        out = jnp.where(lane == k, c.astype(dtype), out)
    return out


def _norm_router_kernel(x_ref, g_ref, shift_ref, scale_ref, wr_ref, br_ref,
                        hrows_ref, idx_ref, gate_ref, cnt_ref):
    tm, d = x_ref.shape
    h = _normed(x_ref[...], g_ref[...], shift_ref[...], scale_ref[...])
    nchunk = d // LANES
    for c in range(nchunk):
        hrows_ref[pl.ds(c, tm, stride=nchunk), :] = h[:, c * LANES:(c + 1) * LANES]
    logits = _dot_f32(h, wr_ref[...]) + br_ref[...]
    ne = logits.shape[-1]
    eidx = lax.broadcasted_iota(I32, (tm, ne), 1).astype(F32)
    work = logits
    vals, idxs = [], []
    for _ in range(TOP_K):
        m = jnp.max(work, axis=-1, keepdims=True)
        sel = jnp.min(jnp.where(work == m, eidx, float(ne)), axis=-1, keepdims=True)
        vals.append(m)
        idxs.append(sel)
        work = jnp.where(eidx == sel, -jnp.inf, work)
    es = [jnp.exp(v - vals[0]) for v in vals]
    tot = es[0]
    for e in es[1:]:
        tot = tot + e
    gates = [e / tot for e in es]
    idx_ref[...] = _lane_place(idxs, (tm, LANES), F32).astype(I32)
    gate_ref[...] = _lane_place(gates, (tm, LANES), F32)
    lane = lax.broadcasted_iota(I32, (tm, LANES), 1).astype(F32)
    onehot = jnp.zeros((tm, LANES), F32)
    for sel in idxs:
        onehot = onehot + jnp.where(lane == sel, 1.0, 0.0)

    @pl.when(pl.program_id(0) == 0)
    def _():
        cnt_ref[...] = jnp.zeros_like(cnt_ref)

    cnt_ref[...] += jnp.sum(onehot, axis=0, keepdims=True)


def _norm_router(x, g4, mod, slot, w_router, b_router3, layer, tm):
    m, d = x.shape
    ne = w_router.shape[-1]
    nchunk = d // LANES
    return pl.pallas_call(
        _norm_router_kernel,
        grid=(m // tm,),
        in_specs=[
            pl.BlockSpec((tm, d), lambda i: (i, 0)),
            pl.BlockSpec((None, 1, d), lambda i: (slot, 0, 0)),
            mod.spec(slot, 0, tm, d, 0),
            mod.spec(slot, 1, tm, d, 0),
            pl.BlockSpec((None, d, ne), lambda i: (layer, 0, 0)),
            pl.BlockSpec((None, 1, ne), lambda i: (layer, 0, 0)),
        ],
        out_specs=[
            pl.BlockSpec((tm * nchunk, LANES), lambda i: (i, 0)),
            pl.BlockSpec((tm, LANES), lambda i: (i, 0)),
            pl.BlockSpec((tm, LANES), lambda i: (i, 0)),
            pl.BlockSpec((1, LANES), lambda i: (0, 0)),
        ],
        out_shape=[
            jax.ShapeDtypeStruct((m * nchunk, LANES), F32),
            jax.ShapeDtypeStruct((m, LANES), I32),
            jax.ShapeDtypeStruct((m, LANES), F32),
            jax.ShapeDtypeStruct((1, LANES), F32),
        ],
        compiler_params=_cparams(("arbitrary",)),
        name="norm_router",
    )(x, g4, mod.arr, mod.arr, w_router, b_router3)


def _head_norm(acc, g, post_scale):
    outs = []
    for hh in range(acc.shape[-1] // HEAD_DIM):
        blk = acc[:, hh * HEAD_DIM:(hh + 1) * HEAD_DIM]
        y = blk * lax.rsqrt(jnp.mean(blk * blk, axis=-1, keepdims=True) + NORM_EPS)
        outs.append(y * g * post_scale)
    return jnp.concatenate(outs, axis=-1)


def _proj_kernel(mode, nw, h_ref, *refs):
    w_refs = refs[:nw]
    rest = refs[nw:]
    h = h_ref[...]
    accs = [jnp.dot(h, w[...].astype(BF16), preferred_element_type=F32) for w in w_refs]
    if mode == "q":
        g_ref, o_ref = rest
        o_ref[...] = _head_norm(accs[0], g_ref[...], 1.0 / math.sqrt(HEAD_DIM)).astype(o_ref.dtype)
    elif mode == "k":
        g_ref, o_ref, ob_ref = rest
        k = _head_norm(accs[0], g_ref[...], 1.0)
        o_ref[...] = k
        ob_ref[...] = k.astype(ob_ref.dtype)
    elif mode == "v":
        o_ref, ob_ref = rest
        o_ref[...] = accs[0]
        ob_ref[...] = accs[0].astype(ob_ref.dtype)
    elif mode == "glu":
        (o_ref,) = rest
        o_ref[...] = accs[0] * jax.nn.sigmoid(accs[1])
    elif mode == "mul":
        (o_ref,) = rest
        o_ref[...] = (accs[0] * accs[1]).astype(o_ref.dtype)
    else:
        (o_ref,) = rest
        o_ref[...] = accs[0].astype(o_ref.dtype)


def _proj(h, w3, layer, col_offsets, ncols, mode, out_dtypes, g=None):
    m, kdim = h.shape
    tm = min(PROJ_TM, m)
    tn = PROJ_TN
    nw = len(col_offsets)
    in_specs = [pl.BlockSpec((tm, kdim), lambda j, i: (i, 0))]
    args = [h]
    for off in col_offsets:
        in_specs.append(pl.BlockSpec((None, kdim, tn), lambda j, i, off=off: (layer, 0, off // tn + j)))
        args.append(w3)
    if g is not None:
        in_specs.append(pl.BlockSpec((None, 1, HEAD_DIM), lambda j, i: (layer, 0, 0)))
        args.append(g)
    out_specs = [pl.BlockSpec((tm, tn), lambda j, i: (i, j)) for _ in out_dtypes]
    out_shape = [jax.ShapeDtypeStruct((m, ncols), dt) for dt in out_dtypes]
    res = pl.pallas_call(
        functools.partial(_proj_kernel, mode, nw),
        grid=(ncols // tn, m // tm),
        in_specs=in_specs,
        out_specs=out_specs,
        out_shape=out_shape,
        compiler_params=_cparams(("arbitrary", "arbitrary")),
        name="proj_" + mode,
    )(*args)
    return res


def _out_proj_kernel(nl, *refs):
    lhs_refs = refs[:nl]
    w_refs = refs[nl:2 * nl]
    x_ref, gate_ref, o_ref = refs[2 * nl:]
    acc = None
    for l_ref, w_ref in zip(lhs_refs, w_refs):
        part = jnp.dot(l_ref[...], w_ref[...].astype(BF16), preferred_element_type=F32)
        acc = part if acc is None else acc + part
    o_ref[...] = x_ref[...] + gate_ref[...] * acc


def _out_proj(lhs_list, w3, layer, x, mod, slot):
    m, d = x.shape
    tm = min(PROJ_TM, m)
    tn = PROJ_TN
    nl = len(lhs_list)
    in_specs, args = [], []
    for lhs in lhs_list:
        in_specs.append(pl.BlockSpec((tm, lhs.shape[1]), lambda j, i: (i, 0)))
        args.append(lhs)
    row = 0
    for lhs in lhs_list:
        kp = lhs.shape[1]
        in_specs.append(pl.BlockSpec((None, kp, tn), lambda j, i, r=row // kp: (layer, r, j)))
        args.append(w3)
        row += kp
    in_specs.append(pl.BlockSpec((tm, tn), lambda j, i: (i, j)))
    args.append(x)
    in_specs.append(mod.spec(slot, 2, tm, tn, 1, col_axis=0))
    args.append(mod.arr)
    return pl.pallas_call(
        functools.partial(_out_proj_kernel, nl),
        grid=(d // tn, m // tm),
        in_specs=in_specs,
        out_specs=pl.BlockSpec((tm, tn), lambda j, i: (i, j)),
        out_shape=jax.ShapeDtypeStruct((m, d), F32),
        compiler_params=_cparams(("arbitrary", "arbitrary")),
        name="out_proj",
    )(*args)


def _log_gates(z):
    sp = jnp.maximum(z, 0.0) + jnp.log1p(jnp.exp(-jnp.abs(z)))
    return -sp, z - sp


def _attn_prompt_kernel(bias_ref, q_ref, k_ref, v_ref, o_ref):
    bq = q_ref.shape[0]
    bk = bq
    qi = pl.program_id(1)
    nheads = q_ref.shape[1] // HEAD_DIM
    row = lax.broadcasted_iota(I32, (bk, bk), 0)
    col = lax.broadcasted_iota(I32, (bk, bk), 1)
    after_mat = jnp.where(row > col, 1.0, 0.0).astype(BF16)
    causal = col < row
    nt = (((1,), (1,)), ((), ()))

    for hh in range(nheads):
        cols = slice(hh * HEAD_DIM, (hh + 1) * HEAD_DIM)
        qh = q_ref[:, cols]
        bias = bias_ref[0, hh]

        def block(kj, vj, run, acc, masked):
            z = lax.dot_general(qh, kj, nt, preferred_element_type=F32) + bias
            lk, lb = _log_gates(z)
            if masked:
                lk = jnp.where(causal, lk, 0.0)
            after = jnp.dot(lk.astype(BF16), after_mat, preferred_element_type=F32)
            a = jnp.exp(lb + after + run)
            if masked:
                a = jnp.where(causal, a, 0.0)
            acc = acc + jnp.dot(a.astype(BF16), vj, preferred_element_type=F32)
            run = run + jnp.sum(lk, axis=-1, keepdims=True)
            return run, acc

        start = pl.multiple_of(qi * bk, bk)
        run, acc = block(k_ref[pl.ds(start, bk), cols], v_ref[pl.ds(start, bk), cols],
                         jnp.zeros((bq, 1), F32), jnp.zeros((bq, HEAD_DIM), F32), True)

        def body(step, carry):
            s0 = pl.multiple_of((qi - 1 - step) * bk, bk)
            return block(k_ref[pl.ds(s0, bk), cols], v_ref[pl.ds(s0, bk), cols], carry[0], carry[1], False)

        run, acc = lax.fori_loop(0, qi, body, (run, acc))
        o_ref[:, cols] = acc.astype(o_ref.dtype)


def _attn_prompt(q, k, v, sb_bias2, layer, nbatch, seq):
    m, width = q.shape
    bq = min(ATTN_BQ, seq)
    nq = seq // bq
    return pl.pallas_call(
        _attn_prompt_kernel,
        grid=(nbatch, nq),
        in_specs=[
            pl.BlockSpec((1, sb_bias2.shape[1]), lambda b, i: (layer, 0), memory_space=pltpu.SMEM),
            pl.BlockSpec((bq, width), lambda b, i: (b * nq + i, 0)),
            pl.BlockSpec((seq, width), lambda b, i: (b, 0)),
            pl.BlockSpec((seq, width), lambda b, i: (b, 0)),
        ],
        out_specs=pl.BlockSpec((bq, width), lambda b, i: (b * nq + i, 0)),
        out_shape=jax.ShapeDtypeStruct((m, width), BF16),
        compiler_params=_cparams(("arbitrary", "arbitrary")),
        name="attn_prompt",
    )(sb_bias2, q, k, v)


def _attn_decode_kernel(npages, pt_ref, q_ref, bias_ref, *refs):
    del pt_ref
    k_refs = refs[:npages]
    v_refs = refs[npages:2 * npages]
    o_ref, run_ref, acc_ref = refs[2 * npages:]
    grp = pl.program_id(1)

    @pl.when(grp == 0)
    def _():
        run_ref[...] = jnp.zeros_like(run_ref)
        acc_ref[...] = jnp.zeros_like(acc_ref)

    q = q_ref[...]
    bias = bias_ref[...]
    psize = k_refs[0].shape[0]
    run = run_ref[...]
    acc = acc_ref[...]
    for j in reversed(range(npages)):
        kj = k_refs[j][...]
        z = jnp.sum(kj * q[None], axis=-1, keepdims=True) + bias[None]
        lk, lb = _log_gates(z)
        incl = lk
        shift = 1
        while shift < psize:
            moved = jnp.concatenate([incl[shift:], jnp.zeros((shift,) + incl.shape[1:], F32)], axis=0)
            incl = incl + moved
            shift *= 2
        a = jnp.exp(lb + (incl - lk) + run[None])
        acc = acc + jnp.sum(a * v_refs[j][...], axis=0)
        run = run + incl[0]
    run_ref[...] = run
    acc_ref[...] = acc

    @pl.when(grp == pl.num_programs(1) - 1)
    def _():
        o_ref[...] = acc


def _attn_decode(q3, sb_bias_col, cache_k, cache_v, page_table, layer):
    nseq, nheads, hd = q3.shape
    npages = page_table.shape[1]
    psize = cache_k.shape[2]
    per_step = min(DECODE_PAGES_PER_STEP, npages)
    ngrp = npages // per_step
    page_specs = []
    for j in range(per_step):
        page_specs.append(pl.BlockSpec(
            (None, None, psize, nheads, hd),
            lambda b, g, pt, j=j: (layer, pt[b, (ngrp - 1 - g) * per_step + j], 0, 0, 0)))
    grid_spec = pltpu.PrefetchScalarGridSpec(
        num_scalar_prefetch=1,
        grid=(nseq, ngrp),
        in_specs=[
            pl.BlockSpec((None, nheads, hd), lambda b, g, pt: (b, 0, 0)),
            pl.BlockSpec((None, nheads, 1), lambda b, g, pt: (layer, 0, 0)),
        ] + page_specs + page_specs,
        out_specs=pl.BlockSpec((None, nheads, hd), lambda b, g, pt: (b, 0, 0)),
        scratch_shapes=[pltpu.VMEM((nheads, 1), F32), pltpu.VMEM((nheads, hd), F32)],
    )
    return pl.pallas_call(
        functools.partial(_attn_decode_kernel, per_step),
        grid_spec=grid_spec,
        out_shape=jax.ShapeDtypeStruct((nseq, nheads, hd), F32),
        compiler_params=_cparams(("arbitrary", "arbitrary")),
        name="attn_decode",
    )(page_table, q3, sb_bias_col, *([cache_k] * per_step), *([cache_v] * per_step))


def _ln_swish(conv, bias, g, b):
    y = conv + bias
    mu = jnp.mean(y, axis=-1, keepdims=True)
    yc = y - mu
    var = jnp.mean(yc * yc, axis=-1, keepdims=True)
    return _silu(yc * lax.rsqrt(var + NORM_EPS) * g + b)


def _convb_prompt_kernel(u_ref, halo_ref, w_ref, bias_ref, g_ref, b_ref, y_ref, full_ref):
    tt = u_ref.shape[0]
    width = w_ref.shape[0]
    first = pl.program_id(1) == 0
    full_ref[0:CONV_HALO, :] = jnp.where(first, 0.0, halo_ref[...])
    full_ref[CONV_HALO:, :] = u_ref[...]
    base = CONV_HALO - (width - 1)
    acc = jnp.zeros(u_ref.shape, F32)
    for k in range(width):
        acc = acc + w_ref[k:k + 1, :] * full_ref[pl.ds(base + k, tt), :]
    y_ref[...] = _ln_swish(acc, bias_ref[...], g_ref[...], b_ref[...]).astype(y_ref.dtype)


def _convb_prompt(u, conv_w, conv_bias, ln_g, ln_b, layer, nbatch, seq):
    m, c = u.shape
    tt = min(CONV_TT, seq)
    nt = seq // tt
    width = conv_w.shape[1]
    per = tt // CONV_HALO
    vec = lambda b, t: (layer, 0, 0)
    return pl.pallas_call(
        _convb_prompt_kernel,
        grid=(nbatch, nt),
        in_specs=[
            pl.BlockSpec((tt, c), lambda b, t: (b * nt + t, 0)),
            pl.BlockSpec((CONV_HALO, c), lambda b, t: (jnp.maximum((b * nt + t) * per - 1, 0), 0)),
            pl.BlockSpec((None, width, c), vec),
            pl.BlockSpec((None, 1, c), vec),
            pl.BlockSpec((None, 1, c), vec),
            pl.BlockSpec((None, 1, c), vec),
        ],
        out_specs=pl.BlockSpec((tt, c), lambda b, t: (b * nt + t, 0)),
        out_shape=jax.ShapeDtypeStruct((m, c), BF16),
        scratch_shapes=[pltpu.VMEM((tt + CONV_HALO, c), F32)],
        compiler_params=_cparams(("arbitrary", "arbitrary")),
        name="convb_prompt",
    )(u, u, conv_w, conv_bias, ln_g, ln_b)


def _convb_sample_kernel(hist_ref, u_ref, w_ref, bias_ref, g_ref, b_ref, y_ref, nh_ref):
    nhist = hist_ref.shape[0]
    hist = hist_ref[...]
    un = u_ref[...]
    conv = jnp.sum(w_ref[0:nhist, :] * hist, axis=0, keepdims=True) + w_ref[nhist:nhist + 1, :] * un
    y_ref[...] = _ln_swish(conv, bias_ref[...], g_ref[...], b_ref[...]).astype(y_ref.dtype)
    nh_ref[0:nhist - 1, :] = hist[1:, :]
    nh_ref[nhist - 1:nhist, :] = un


def _convb_sample(hist, u3, conv_w, conv_bias, ln_g, ln_b, layer):
    nseq, nhist, c = hist.shape[1:]
    width = conv_w.shape[1]
    vec = lambda b: (layer, 0, 0)
    return pl.pallas_call(
        _convb_sample_kernel,
        grid=(nseq,),
        in_specs=[
            pl.BlockSpec((None, None, nhist, c), lambda b: (layer, b, 0, 0)),
            pl.BlockSpec((None, 1, c), lambda b: (b, 0, 0)),
            pl.BlockSpec((None, width, c), vec),
            pl.BlockSpec((None, 1, c), vec),
            pl.BlockSpec((None, 1, c), vec),
            pl.BlockSpec((None, 1, c), vec),
        ],
        out_specs=[
            pl.BlockSpec((None, 1, c), lambda b: (b, 0, 0)),
            pl.BlockSpec((None, nhist, c), lambda b: (b, 0, 0)),
        ],
        out_shape=[
            jax.ShapeDtypeStruct((nseq, 1, c), BF16),
            jax.ShapeDtypeStruct((nseq, nhist, c), F32),
        ],
        compiler_params=_cparams(("arbitrary",)),
        name="convb_sample",
    )(hist, u3, conv_w, conv_bias, ln_g, ln_b)


def _shortconv_prompt_kernel(gb_ref, cv_ref, halo_ref, w_ref, o_ref, full_ref):
    tt = cv_ref.shape[0]
    width = w_ref.shape[0]
    first = pl.program_id(1) == 0
    full_ref[0:SHORT_HALO, :] = jnp.where(first, 0.0, halo_ref[...].astype(F32))
    full_ref[SHORT_HALO:, :] = cv_ref[...].astype(F32)
    base = SHORT_HALO - (width - 1)
    acc = jnp.zeros(cv_ref.shape, F32)
    for k in range(width):
        acc = acc + w_ref[k:k + 1, :] * full_ref[pl.ds(base + k, tt), :]
    o_ref[...] = (gb_ref[...].astype(F32) * acc).astype(o_ref.dtype)


def _shortconv_prompt(gate_b, cv, conv_w, layer, nbatch, seq):
    m, c = cv.shape
    tt = min(CONV_TT, seq)
    nt = seq // tt
    width = conv_w.shape[1]
    per = tt // SHORT_HALO
    return pl.pallas_call(
        _shortconv_prompt_kernel,
        grid=(nbatch, nt),
        in_specs=[
            pl.BlockSpec((tt, c), lambda b, t: (b * nt + t, 0)),
            pl.BlockSpec((tt, c), lambda b, t: (b * nt + t, 0)),
            pl.BlockSpec((SHORT_HALO, c), lambda b, t: (jnp.maximum((b * nt + t) * per - 1, 0), 0)),
            pl.BlockSpec((None, width, c), lambda b, t: (layer, 0, 0)),
        ],
        out_specs=pl.BlockSpec((tt, c), lambda b, t: (b * nt + t, 0)),
        out_shape=jax.ShapeDtypeStruct((m, c), BF16),
        scratch_shapes=[pltpu.VMEM((tt + SHORT_HALO, c), F32)],
        compiler_params=_cparams(("arbitrary", "arbitrary")),
        name="shortconv_prompt",
    )(gate_b, cv, cv, conv_w)


def _shortconv_sample_kernel(gb_ref, cv_ref, h0_ref, h1_ref, w_ref, o_ref):
    conv = w_ref[0:1, :] * h0_ref[...] + w_ref[1:2, :] * h1_ref[...] + w_ref[2:3, :] * cv_ref[...]
    o_ref[...] = (gb_ref[...].astype(F32) * conv).astype(o_ref.dtype)


def _shortconv_sample(gate_b, cv, h0, h1, conv_w, layer):
    m, c = cv.shape
    width = conv_w.shape[1]
    full = pl.BlockSpec((m, c), lambda i: (0, 0))
    return pl.pallas_call(
        _shortconv_sample_kernel,
        grid=(1,),
        in_specs=[full, full, full, full, pl.BlockSpec((None, width, c), lambda i: (layer, 0, 0))],
        out_specs=full,
        out_shape=jax.ShapeDtypeStruct((m, c), BF16),
        compiler_params=_cparams(("arbitrary",)),
        name="shortconv_sample",
    )(gate_b, cv, h0, h1, conv_w)


def _rank_kernel(idx_ref, start_ref, dest_ref, run_ref):
    tm = idx_ref.shape[0]

    @pl.when(pl.program_id(0) == 0)
    def _():
        run_ref[...] = start_ref[...]

    idxf = idx_ref[...].astype(F32)
    lane = lax.broadcasted_iota(I32, (tm, LANES), 1).astype(F32)
    onehots = [jnp.where(lane == idxf[:, k:k + 1], 1.0, 0.0) for k in range(TOP_K)]
    cnt = onehots[0]
    for oh in onehots[1:]:
        cnt = cnt + oh
    row = lax.broadcasted_iota(I32, (tm, tm), 0)
    col = lax.broadcasted_iota(I32, (tm, tm), 1)
    earlier = jnp.where(col < row, 1.0, 0.0).astype(BF16)
    before = jnp.dot(earlier, cnt.astype(BF16), preferred_element_type=F32)
    tot = run_ref[...] + before
    dests = [jnp.sum(oh * tot, axis=-1, keepdims=True) for oh in onehots]
    dest_ref[...] = _lane_place(dests, (tm, LANES), F32).astype(I32)
    run_ref[...] += jnp.sum(cnt, axis=0, keepdims=True)


def _rank(idx_all, start_rows):
    m = idx_all.shape[0]
    tm = ROW_TILE
    return pl.pallas_call(
        _rank_kernel,
        grid=(m // tm,),
        in_specs=[pl.BlockSpec((tm, LANES), lambda i: (i, 0)), pl.BlockSpec((1, LANES), lambda i: (0, 0))],
        out_specs=pl.BlockSpec((tm, LANES), lambda i: (i, 0)),
        out_shape=jax.ShapeDtypeStruct((m, LANES), I32),
        scratch_shapes=[pltpu.VMEM((1, LANES), F32)],
        compiler_params=_cparams(("arbitrary",)),
        name="moe_rank",
    )(idx_all, start_rows)


def _dispatch_kernel(nchunk, dest_ref, hrows_ref, xs_in_ref, xs_ref, sem):
    del xs_in_ref
    tm = hrows_ref.shape[0] // nchunk

    def issue(t, carry):
        src = hrows_ref.at[pl.ds(pl.multiple_of(t * nchunk, nchunk), nchunk)]
        for k in range(TOP_K):
            d = dest_ref[0, 0, t * TOP_K + k]
            dst = xs_ref.at[pl.ds(pl.multiple_of(d * nchunk, nchunk), nchunk)]
            pltpu.make_async_copy(src, dst, sem).start()
        return carry

    lax.fori_loop(0, tm, issue, 0)

    def drain(t, carry):
        src = hrows_ref.at[pl.ds(0, nchunk)]
        for _ in range(TOP_K):
            pltpu.make_async_copy(src, xs_ref.at[pl.ds(0, nchunk)], sem).wait()
        return carry

    lax.fori_loop(0, tm, drain, 0)


def _dispatch(dest3, hrows, xs, nchunk):
    ntile = dest3.shape[0]
    tm = ROW_TILE
    return pl.pallas_call(
        functools.partial(_dispatch_kernel, nchunk),
        grid=(ntile,),
        in_specs=[
            pl.BlockSpec((1, 1, tm * TOP_K), lambda i: (i, 0, 0), memory_space=pltpu.SMEM),
            pl.BlockSpec((tm * nchunk, LANES), lambda i: (i, 0)),
            pl.BlockSpec(memory_space=pl.ANY),
        ],
        out_specs=pl.BlockSpec(memory_space=pl.ANY),
        out_shape=jax.ShapeDtypeStruct(xs.shape, xs.dtype),
        scratch_shapes=[pltpu.SemaphoreType.DMA],
        input_output_aliases={2: 0},
        compiler_params=_cparams(("arbitrary",)),
        name="moe_dispatch",
    )(dest3, hrows, xs)


def _clamped_swiglu(gate, up):
    gate = jnp.minimum(gate, SWIGLU_LIMIT)
    up = jnp.clip(up, -SWIGLU_LIMIT, SWIGLU_LIMIT)
    return (up + 1.0) * gate * jax.nn.sigmoid(SWIGLU_ALPHA * gate)


def _experts_kernel(nchunk, blk_e_ref, nused_ref, x_ref, wg_ref, wu_ref, bg_ref, bu_ref, wd_ref, bd_ref,
                    y_ref, xb_ref, acc_ref):
    del blk_e_ref
    b = pl.program_id(0)
    f = pl.program_id(1)
    bm = xb_ref.shape[0]

    @pl.when(b < nused_ref[0])
    def _():
        @pl.when(f == 0)
        def _():
            for c in range(nchunk):
                xb_ref[:, c * LANES:(c + 1) * LANES] = x_ref[pl.ds(c, bm, stride=nchunk), :].astype(BF16)
            acc_ref[...] = jnp.zeros_like(acc_ref)

        xb = xb_ref[...]
        gate = jnp.dot(xb, wg_ref[...].astype(BF16), preferred_element_type=F32) + bg_ref[...]
        up = jnp.dot(xb, wu_ref[...].astype(BF16), preferred_element_type=F32) + bu_ref[...]
        act = _clamped_swiglu(gate, up).astype(BF16)
        acc_ref[...] += jnp.dot(act, wd_ref[...].astype(BF16), preferred_element_type=F32)

        @pl.when(f == pl.num_programs(1) - 1)
        def _():
            y = acc_ref[...] + bd_ref[...]
            for c in range(nchunk):
                y_ref[pl.ds(c, bm, stride=nchunk), :] = y[:, c * LANES:(c + 1) * LANES]

    @pl.when((b >= nused_ref[0]) & (f == 0))
    def _():
        y_ref[...] = jnp.zeros_like(y_ref)


def _experts(blk_e, nused, xs, w_gu, b_gu4, w_dn, b_dn4, layer, nchunk):
    d = nchunk * LANES
    nrows = xs.shape[0] // nchunk
    bm = MOE_BLOCK
    tf = MOE_FF_TILE
    dff = w_dn.shape[2]
    nf = dff // tf
    nblk = nrows // bm

    def blk(b, nu):
        return jnp.minimum(b, nu[0] - 1)

    grid_spec = pltpu.PrefetchScalarGridSpec(
        num_scalar_prefetch=2,
        grid=(nblk, nf),
        in_specs=[
            pl.BlockSpec((bm * nchunk, LANES), lambda b, f, be, nu: (blk(b, nu), 0)),
            pl.BlockSpec((None, None, d, tf), lambda b, f, be, nu: (layer, be[blk(b, nu)], 0, f)),
            pl.BlockSpec((None, None, d, tf), lambda b, f, be, nu: (layer, be[blk(b, nu)], 0, nf + f)),
            pl.BlockSpec((None, None, 1, tf), lambda b, f, be, nu: (layer, be[blk(b, nu)], 0, f)),
            pl.BlockSpec((None, None, 1, tf), lambda b, f, be, nu: (layer, be[blk(b, nu)], 0, nf + f)),
            pl.BlockSpec((None, None, tf, d), lambda b, f, be, nu: (layer, be[blk(b, nu)], f, 0)),
            pl.BlockSpec((None, None, 1, d), lambda b, f, be, nu: (layer, be[blk(b, nu)], 0, 0)),
        ],
        out_specs=pl.BlockSpec((bm * nchunk, LANES), lambda b, f, be, nu: (b, 0)),
        scratch_shapes=[pltpu.VMEM((bm, d), BF16), pltpu.VMEM((bm, d), F32)],
    )
    return pl.pallas_call(
        functools.partial(_experts_kernel, nchunk),
        grid_spec=grid_spec,
        out_shape=jax.ShapeDtypeStruct(xs.shape, F32),
        compiler_params=_cparams(("arbitrary", "arbitrary")),
        name="moe_experts",
    )(blk_e, nused, xs, w_gu, w_gu, b_gu4, b_gu4, w_dn, b_dn4)


def _combine_kernel(nchunk, dest_ref, x_ref, gate_ref, g_ref, ys_ref, o_ref, buf_ref, sem):
    tm = x_ref.shape[0]

    def issue(t, carry):
        for k in range(TOP_K):
            d = dest_ref[0, 0, t * TOP_K + k]
            src = ys_ref.at[pl.ds(pl.multiple_of(d * nchunk, nchunk), nchunk)]
            dst = buf_ref.at[k, pl.ds(pl.multiple_of(t * nchunk, nchunk), nchunk)]
            pltpu.make_async_copy(src, dst, sem).start()
        return carry

    lax.fori_loop(0, tm, issue, 0)

    def drain(t, carry):
        for k in range(TOP_K):
            pltpu.make_async_copy(ys_ref.at[pl.ds(0, nchunk)], buf_ref.at[k, pl.ds(0, nchunk)], sem).wait()
        return carry

    lax.fori_loop(0, tm, drain, 0)

    g = g_ref[...]
    for c in range(nchunk):
        cols = slice(c * LANES, (c + 1) * LANES)
        mix = None
        for k in range(TOP_K):
            part = g[:, k:k + 1] * buf_ref[k, pl.ds(c, tm, stride=nchunk), :]
            mix = part if mix is None else mix + part
        o_ref[:, cols] = x_ref[:, cols] + gate_ref[:, cols] * mix


def _combine(dest3, x, mod, slot, gates, ys, nchunk):
    m, d = x.shape
    tm = ROW_TILE
    return pl.pallas_call(
        functools.partial(_combine_kernel, nchunk),
        grid=(m // tm,),
        in_specs=[
            pl.BlockSpec((1, 1, tm * TOP_K), lambda i: (i, 0, 0), memory_space=pltpu.SMEM),
            pl.BlockSpec((tm, d), lambda i: (i, 0)),
            mod.spec(slot, 2, tm, d, 0),
            pl.BlockSpec((tm, LANES), lambda i: (i, 0)),
            pl.BlockSpec(memory_space=pl.ANY),
        ],
        out_specs=pl.BlockSpec((tm, d), lambda i: (i, 0)),
        out_shape=jax.ShapeDtypeStruct((m, d), F32),
        scratch_shapes=[pltpu.VMEM((TOP_K, tm * nchunk, LANES), F32), pltpu.SemaphoreType.DMA],
        compiler_params=_cparams(("arbitrary",)),
        name="moe_combine",
    )(dest3, x, mod.arr, gates, ys)


def _moe(xs_groups, mods, g4, slot, layer, w_router, b_router3, w_gu, b_gu4, w_dn, b_dn4):
    d = xs_groups[0].shape[1]
    nchunk = d // LANES
    tm = ROW_TILE
    routed = [_norm_router(x, g4, mod, slot, w_router, b_router3, layer, tm)
              for x, mod in zip(xs_groups, mods)]
    counts = sum(r[3] for r in routed)[0, :N_EXPERTS].astype(I32)
    npairs = sum(x.shape[0] for x in xs_groups) * TOP_K
    nblk = (npairs + N_EXPERTS * (MOE_BLOCK - 1)) // MOE_BLOCK
    blocks = (counts + MOE_BLOCK - 1) // MOE_BLOCK
    ends = jnp.cumsum(blocks)
    starts = ends - blocks
    start_rows = jnp.zeros((1, LANES), F32).at[0, :N_EXPERTS].set((starts * MOE_BLOCK).astype(F32))
    blk_e = jnp.minimum(jnp.searchsorted(ends, jnp.arange(nblk, dtype=I32), side="right"),
                        N_EXPERTS - 1).astype(I32)
    nused = ends[-1:].astype(I32)

    idx_all = jnp.concatenate([r[1] for r in routed], axis=0)
    dest_all = _rank(idx_all, start_rows)
    xs = jnp.zeros((nblk * MOE_BLOCK * nchunk, LANES), F32)
    dests, off = [], 0
    for x, r in zip(xs_groups, routed):
        m = x.shape[0]
        dest3 = dest_all[off:off + m, :TOP_K].reshape(m // tm, 1, tm * TOP_K)
        dests.append(dest3)
        xs = _dispatch(dest3, r[0], xs, nchunk)
        off += m
    ys = _experts(blk_e, nused, xs, w_gu, b_gu4, w_dn, b_dn4, layer, nchunk)
    return [_combine(dest3, x, mod, slot, r[2], ys, nchunk)
            for x, mod, r, dest3 in zip(xs_groups, mods, routed, dests)]


def kernel(x_prompt, x_sample, c_prompt, c_sample, cache_k, cache_v, state_conv_b, state_conv_c, page_table,
           norm_g, w_ada, b_ada, w_in_a, g_q, g_k, sb_bias, conv_b_w, conv_b_bias, ln_b_g, ln_b_b, w_out_a,
           w_in_c, conv_c_w, w_out_c, w_router, b_router, w_gate_up, b_gate_up, w_down, b_down):
    nbatch, seq, d = x_prompt.shape
    nsamp = x_sample.shape[0]
    depth = norm_g.shape[0]
    a_width = g_q.shape[-1] * sb_bias.shape[-1]
    nheads = sb_bias.shape[-1]
    b_width = conv_b_w.shape[-1]
    c_width = conv_c_w.shape[-1]

    xp = x_prompt.reshape(nbatch * seq, d)
    xs_ = x_sample.reshape(nsamp, d)

    mod_all = _ada_all(jnp.concatenate([c_prompt, c_sample], axis=0), w_ada, b_ada)
    mod_p = _Mod(mod_all[:, :nbatch].reshape(mod_all.shape[0], nbatch, 1, 3 * d), False, seq, d)
    mod_s = _Mod(mod_all[:, nbatch:], True, 1, d)
    g4 = norm_g.reshape(-1, 1, d)

    g_q3 = g_q.reshape(-1, 1, HEAD_DIM)
    g_k3 = g_k.reshape(-1, 1, HEAD_DIM)
    sb_col = sb_bias.reshape(-1, nheads, 1)
    conv_b_bias3 = conv_b_bias.reshape(-1, 1, b_width)
    ln_g3 = ln_b_g.reshape(-1, 1, b_width)
    ln_b3 = ln_b_b.reshape(-1, 1, b_width)
    b_router3 = b_router.reshape(depth, 1, -1)
    b_gu4 = b_gate_up.reshape(depth, N_EXPERTS, 1, -1)
    b_dn4 = b_down.reshape(depth, N_EXPERTS, 1, -1)

    new_k, new_v, new_b, new_c = [[], []], [[], []], [[], []], [[], []]
    for i in range(depth):
        j = i // 2
        slot = 2 * i
        hp = _norm_mod(xp, g4, mod_p, slot, min(PROJ_TM, xp.shape[0]))
        hs = _norm_mod(xs_, g4, mod_s, slot, nsamp)
        if i % 2 == 0:
            outs = []
            for grp, h in enumerate((hp, hs)):
                (q,) = _proj(h, w_in_a, j, [0], a_width, "q", [BF16 if grp == 0 else F32], g_q3)
                k, kb = _proj(h, w_in_a, j, [a_width], a_width, "k", [F32, BF16], g_k3)
                v, vb = _proj(h, w_in_a, j, [2 * a_width], a_width, "v", [F32, BF16])
                (u,) = _proj(h, w_in_a, j, [3 * a_width, 3 * a_width + b_width], b_width, "glu", [F32])
                new_k[grp].append(k)
                new_v[grp].append(v)
                outs.append((q, kb, vb, u))
            q, kb, vb, u = outs[0]
            o_a = _attn_prompt(q, kb, vb, sb_bias, j, nbatch, seq)
            y_b = _convb_prompt(u, conv_b_w, conv_b_bias3, ln_g3, ln_b3, j, nbatch, seq)
            new_b[0].append(u.reshape(nbatch, seq, b_width)[:, seq - (conv_b_w.shape[1] - 1):])
            xp = _out_proj([o_a, y_b], w_out_a, j, xp, mod_p, slot)

            q, _, _, u = outs[1]
            o_s = _attn_decode(q.reshape(nsamp, nheads, HEAD_DIM), sb_col, cache_k, cache_v, page_table, j)
            y_s, hist_b = _convb_sample(state_conv_b, u.reshape(nsamp, 1, b_width), conv_b_w, conv_b_bias3,
                                        ln_g3, ln_b3, j)
            new_b[1].append(hist_b)
            xs_ = _out_proj([o_s.reshape(nsamp, a_width).astype(BF16), y_s.reshape(nsamp, b_width)],
                            w_out_a, j, xs_, mod_s, slot)
        else:
            (gb_p,) = _proj(hp, w_in_c, j, [0], c_width, "id", [BF16])
            (cv_p,) = _proj(hp, w_in_c, j, [c_width, 2 * c_width], c_width, "mul", [BF16])
            lhs_p = _shortconv_prompt(gb_p, cv_p, conv_c_w, j, nbatch, seq)
            nh = conv_c_w.shape[1] - 1
            new_c[0].append(cv_p.reshape(nbatch, seq, c_width)[:, seq - nh:].astype(F32))
            xp = _out_proj([lhs_p], w_out_c, j, xp, mod_p, slot)

            (gb_s,) = _proj(hs, w_in_c, j, [0], c_width, "id", [BF16])
            (cv_s,) = _proj(hs, w_in_c, j, [c_width, 2 * c_width], c_width, "mul", [F32])
            h0 = state_conv_c[j, :, 0, :]
            h1 = state_conv_c[j, :, 1, :]
            lhs_s = _shortconv_sample(gb_s, cv_s, h0, h1, conv_c_w, j)
            new_c[1].append(jnp.stack([h1, cv_s], axis=1))
            xs_ = _out_proj([lhs_s], w_out_c, j, xs_, mod_s, slot)

        xp, xs_ = _moe([xp, xs_], [mod_p, mod_s], g4, slot + 1, i, w_router, b_router3,
                       w_gate_up, b_gu4, w_down, b_dn4)

    def heads(a, n, t):
        return a.reshape(n, t, nheads, HEAD_DIM)

    return (
        xp.reshape(nbatch, seq, d),
        xs_.reshape(nsamp, 1, d),
        jnp.stack([heads(k, nbatch, seq) for k in new_k[0]]),
        jnp.stack([heads(v, nbatch, seq) for v in new_v[0]]),
        jnp.stack(new_b[0]),
        jnp.stack(new_c[0]),
        jnp.stack([heads(k, nsamp, 1) for k in new_k[1]]),
        jnp.stack([heads(v, nsamp, 1) for v in new_v[1]]),
        jnp.stack(new_b[1]),
        jnp.stack(new_c[1]),
    )
```

```python
import functools
import math

import jax
import jax.numpy as jnp
from jax import lax
from jax.experimental import pallas as pl
from jax.experimental.pallas import tpu as pltpu

F32 = jnp.float32
BF16 = jnp.bfloat16
I32 = jnp.int32

LANES = 128
SUBLANES = 8
VMEM_LIMIT_BYTES = 56 * 1024 * 1024

HEAD_DIM = 128
TOP_K = 4
N_EXPERTS = 32
SWIGLU_LIMIT = 7.0
SWIGLU_ALPHA = 1.702
NORM_EPS = 1e-6

ROW_TILE = 128
ROUTER_TM = 512
MOE_BLOCK = 512
MOE_FF_TILE = 256
MOE_RUN_BLOCKS = 4
PROJ_TM = 512
PROJ_TN = 512
ATTN_BQ = 256
ATTN_BK = 256
CONV_TT = 256
CONV_HALO = 32
SHORT_HALO = 16
DECODE_PAGES_PER_STEP = 8
ATTN_HEAD_GROUP = 8


def _cparams(sem):
    return pltpu.CompilerParams(dimension_semantics=sem, vmem_limit_bytes=VMEM_LIMIT_BYTES)


def _silu(x):
    return x * jax.nn.sigmoid(x)


def _ada_kernel(c_ref, w_ref, b_ref, o_ref):
    c = c_ref[...]
    s = _silu(c).astype(BF16)
    o_ref[...] = jnp.dot(s, w_ref[...].astype(BF16), preferred_element_type=F32) + b_ref[...]


def _ada_all(c_all, w_ada, b_ada):
    nseq, d = c_all.shape
    nslot = w_ada.shape[0] * w_ada.shape[1]
    n3 = w_ada.shape[-1]
    w4 = w_ada.reshape(nslot, d, n3)
    b4 = b_ada.reshape(nslot, 1, n3)
    tn = 1024
    return pl.pallas_call(
        _ada_kernel,
        grid=(nslot, n3 // tn),
        in_specs=[
            pl.BlockSpec((nseq, d), lambda s, j: (0, 0)),
            pl.BlockSpec((None, d, tn), lambda s, j: (s, 0, j)),
            pl.BlockSpec((None, 1, tn), lambda s, j: (s, 0, j)),
        ],
        out_specs=pl.BlockSpec((None, nseq, tn), lambda s, j: (s, 0, j)),
        out_shape=jax.ShapeDtypeStruct((nslot, nseq, n3), F32),
        compiler_params=_cparams(("arbitrary", "arbitrary")),
        name="ada_mod",
    )(c_all, w4, b4)


class _Mod:
    def __init__(self, arr, per_row, rows_per_seq, d):
        self.arr = arr
        self.per_row = per_row
        self.rows_per_seq = rows_per_seq
        self.d = d

    def spec(self, slot, part, tm, tn, row_axis, col_axis=None, grid_rank=1):
        ncol = self.d // tn

        def col(idx):
            return part * ncol + (idx[col_axis] if col_axis is not None else 0)

        if self.per_row:
            return pl.BlockSpec((None, tm, tn), lambda *idx: (slot, idx[row_axis], col(idx)))
        tps = self.rows_per_seq // tm
        return pl.BlockSpec((None, None, 1, tn), lambda *idx: (slot, idx[row_axis] // tps, 0, col(idx)))


def _normed(x, g, shift, scale):
    y = x * lax.rsqrt(jnp.mean(x * x, axis=-1, keepdims=True) + NORM_EPS)
    return y * g * (1.0 + scale) + shift


def _norm_mod_kernel(x_ref, g_ref, shift_ref, scale_ref, h_ref):
    h_ref[...] = _normed(x_ref[...], g_ref[...], shift_ref[...], scale_ref[...]).astype(h_ref.dtype)


def _norm_mod(x, g4, mod, slot, tm):
    m, d = x.shape
    return pl.pallas_call(
        _norm_mod_kernel,
        grid=(m // tm,),
        in_specs=[
            pl.BlockSpec((tm, d), lambda i: (i, 0)),
            pl.BlockSpec((None, 1, d), lambda i: (slot, 0, 0)),
            mod.spec(slot, 0, tm, d, 0),
            mod.spec(slot, 1, tm, d, 0),
        ],
        out_specs=pl.BlockSpec((tm, d), lambda i: (i, 0)),
        out_shape=jax.ShapeDtypeStruct((m, d), BF16),
        compiler_params=_cparams(("arbitrary",)),
        name="norm_mod",
    )(x, g4, mod.arr, mod.arr)


def _split3(x):
    hi = x.astype(BF16)
    r = x - hi.astype(F32)
    mid = r.astype(BF16)
    lo = (r - mid.astype(F32)).astype(BF16)
    return hi, mid, lo


def _dot_f32(a, b_parts):
    a0, a1, a2 = _split3(a)
    b0, b1, b2 = b_parts
    dot = functools.partial(jnp.dot, preferred_element_type=F32)
    small = dot(a0, b2) + dot(a2, b0) + dot(a1, b1)
    mid = dot(a0, b1) + dot(a1, b0)
    return dot(a0, b0) + (mid + small)


def _lane_place(cols, shape, dtype):
    lane = lax.broadcasted_iota(I32, shape, 1)
    out = jnp.zeros(shape, dtype)
    for k, c in enumerate(cols):
        out = jnp.where(lane == k, c.astype(dtype), out)
    return out


def _norm_router_kernel(x_ref, g_ref, shift_ref, scale_ref, wr_ref, br_ref,
                        hrows_ref, idx_ref, gate_ref, cnt_ref, wparts_ref):
    tm, d = x_ref.shape

    @pl.when(pl.program_id(0) == 0)
    def _():
        for p, part in enumerate(_split3(wr_ref[...])):
            wparts_ref[p] = part
        cnt_ref[...] = jnp.zeros_like(cnt_ref)

    h = _normed(x_ref[...], g_ref[...], shift_ref[...], scale_ref[...])
    nchunk = d // LANES
    for c in range(nchunk):
        hrows_ref[pl.ds(c, tm, stride=nchunk), :] = h[:, c * LANES:(c + 1) * LANES]
    logits = _dot_f32(h, [wparts_ref[p] for p in range(3)]) + br_ref[...]
    ne = logits.shape[-1]
    eidx = lax.broadcasted_iota(I32, (tm, ne), 1).astype(F32)
    work = logits
    vals, idxs = [], []
    for _ in range(TOP_K):
        m = jnp.max(work, axis=-1, keepdims=True)
        sel = jnp.min(jnp.where(work == m, eidx, float(ne)), axis=-1, keepdims=True)
        vals.append(m)
        idxs.append(sel)
        work = jnp.where(eidx == sel, -jnp.inf, work)
    es = [jnp.exp(v - vals[0]) for v in vals]
    tot = es[0]
    for e in es[1:]:
        tot = tot + e
    gates = [e / tot for e in es]
    idx_ref[...] = _lane_place(idxs, (tm, LANES), F32).astype(I32)
    gate_ref[...] = _lane_place(gates, (tm, LANES), F32)
    lane = lax.broadcasted_iota(I32, (tm, LANES), 1).astype(F32)
    onehot = jnp.zeros((tm, LANES), F32)
    for sel in idxs:
        onehot = onehot + jnp.where(lane == sel, 1.0, 0.0)
    cnt_ref[...] += jnp.sum(onehot, axis=0, keepdims=True)


def _norm_router(x, g4, mod, slot, w_router, b_router3, layer, tm):
    m, d = x.shape
    ne = w_router.shape[-1]
    nchunk = d // LANES
    return pl.pallas_call(
        _norm_router_kernel,
        grid=(m // tm,),
        in_specs=[
            pl.BlockSpec((tm, d), lambda i: (i, 0)),
            pl.BlockSpec((None, 1, d), lambda i: (slot, 0, 0)),
            mod.spec(slot, 0, tm, d, 0),
            mod.spec(slot, 1, tm, d, 0),
            pl.BlockSpec((None, d, ne), lambda i: (layer, 0, 0)),
            pl.BlockSpec((None, 1, ne), lambda i: (layer, 0, 0)),
        ],
        out_specs=[
            pl.BlockSpec((tm * nchunk, LANES), lambda i: (i, 0)),
            pl.BlockSpec((tm, LANES), lambda i: (i, 0)),
            pl.BlockSpec((tm, LANES), lambda i: (i, 0)),
            pl.BlockSpec((1, LANES), lambda i: (0, 0)),
        ],
        out_shape=[
            jax.ShapeDtypeStruct((m * nchunk, LANES), F32),
            jax.ShapeDtypeStruct((m, LANES), I32),
            jax.ShapeDtypeStruct((m, LANES), F32),
            jax.ShapeDtypeStruct((1, LANES), F32),
        ],
        scratch_shapes=[pltpu.VMEM((3, d, ne), BF16)],
        compiler_params=_cparams(("arbitrary",)),
        name="norm_router",
    )(x, g4, mod.arr, mod.arr, w_router, b_router3)


def _head_norm(acc, g, post_scale):
    outs = []
    for hh in range(acc.shape[-1] // HEAD_DIM):
        blk = acc[:, hh * HEAD_DIM:(hh + 1) * HEAD_DIM]
        y = blk * lax.rsqrt(jnp.mean(blk * blk, axis=-1, keepdims=True) + NORM_EPS)
        outs.append(y * g * post_scale)
    return jnp.concatenate(outs, axis=-1)


def _proj_kernel(mode, nw, h_ref, *refs):
    w_refs = refs[:nw]
    rest = refs[nw:]
    h = h_ref[...]
    accs = [jnp.dot(h, w[...].astype(BF16), preferred_element_type=F32) for w in w_refs]
    if mode == "q":
        g_ref, o_ref = rest
        o_ref[...] = _head_norm(accs[0], g_ref[...], SCORE_SCALE).astype(o_ref.dtype)
    elif mode == "k":
        g_ref, o_ref, ob_ref = rest
        k = _head_norm(accs[0], g_ref[...], 1.0)
        o_ref[...] = k
        ob_ref[...] = k.astype(ob_ref.dtype)
    elif mode == "v":
        o_ref, ob_ref = rest
        o_ref[...] = accs[0]
        ob_ref[...] = accs[0].astype(ob_ref.dtype)
    elif mode == "glu":
        (o_ref,) = rest
        o_ref[...] = accs[0] * jax.nn.sigmoid(accs[1])
    elif mode == "mul":
        (o_ref,) = rest
        o_ref[...] = (accs[0] * accs[1]).astype(o_ref.dtype)
    else:
        (o_ref,) = rest
        o_ref[...] = accs[0].astype(o_ref.dtype)


def _proj(h, w3, layer, col_offsets, ncols, mode, out_dtypes, g=None):
    m, kdim = h.shape
    tm = min(PROJ_TM, m)
    tn = PROJ_TN
    nw = len(col_offsets)
    in_specs = [pl.BlockSpec((tm, kdim), lambda j, i: (i, 0))]
    args = [h]
    for off in col_offsets:
        in_specs.append(pl.BlockSpec((None, kdim, tn), lambda j, i, off=off: (layer, 0, off // tn + j)))
        args.append(w3)
    if g is not None:
        in_specs.append(pl.BlockSpec((None, 1, HEAD_DIM), lambda j, i: (layer, 0, 0)))
        args.append(g)
    out_specs = [pl.BlockSpec((tm, tn), lambda j, i: (i, j)) for _ in out_dtypes]
    out_shape = [jax.ShapeDtypeStruct((m, ncols), dt) for dt in out_dtypes]
    res = pl.pallas_call(
        functools.partial(_proj_kernel, mode, nw),
        grid=(ncols // tn, m // tm),
        in_specs=in_specs,
        out_specs=out_specs,
        out_shape=out_shape,
        compiler_params=_cparams(("arbitrary", "arbitrary")),
        name="proj_" + mode,
    )(*args)
    return res


def _out_proj_kernel(nl, *refs):
    lhs_refs = refs[:nl]
    w_refs = refs[nl:2 * nl]
    x_ref, gate_ref, o_ref = refs[2 * nl:]
    acc = None
    for l_ref, w_ref in zip(lhs_refs, w_refs):
        part = jnp.dot(l_ref[...], w_ref[...].astype(BF16), preferred_element_type=F32)
        acc = part if acc is None else acc + part
    o_ref[...] = x_ref[...] + gate_ref[...] * acc


def _out_proj(lhs_list, w3, layer, x, mod, slot):
    m, d = x.shape
    tm = min(PROJ_TM, m)
    tn = PROJ_TN
    nl = len(lhs_list)
    in_specs, args = [], []
    for lhs in lhs_list:
        in_specs.append(pl.BlockSpec((tm, lhs.shape[1]), lambda j, i: (i, 0)))
        args.append(lhs)
    row = 0
    for lhs in lhs_list:
        kp = lhs.shape[1]
        in_specs.append(pl.BlockSpec((None, kp, tn), lambda j, i, r=row // kp: (layer, r, j)))
        args.append(w3)
        row += kp
    in_specs.append(pl.BlockSpec((tm, tn), lambda j, i: (i, j)))
    args.append(x)
    in_specs.append(mod.spec(slot, 2, tm, tn, 1, col_axis=0))
    args.append(mod.arr)
    return pl.pallas_call(
        functools.partial(_out_proj_kernel, nl),
        grid=(d // tn, m // tm),
        in_specs=in_specs,
        out_specs=pl.BlockSpec((tm, tn), lambda j, i: (i, j)),
        out_shape=jax.ShapeDtypeStruct((m, d), F32),
        compiler_params=_cparams(("arbitrary", "arbitrary")),
        name="out_proj",
    )(*args)


LOG2E = 1.4426950408889634
SCORE_SCALE = -LOG2E / math.sqrt(HEAD_DIM)


def _log_gates(nz):
    l2 = jnp.log(1.0 + jnp.exp2(-jnp.abs(nz))) * LOG2E
    lk = jnp.minimum(nz, 0.0) - l2
    return lk, lk - nz


def _attn_prompt_kernel(bias_ref, q_ref, k_ref, v_ref, o_ref):
    bq = q_ref.shape[0]
    bk = bq
    qi = pl.program_id(1)
    nheads = q_ref.shape[1] // HEAD_DIM
    row = lax.broadcasted_iota(I32, (bk, bk), 0)
    col = lax.broadcasted_iota(I32, (bk, bk), 1)
    after_mat = jnp.where(row > col, 1.0, 0.0).astype(BF16)
    causal = col < row
    nt = (((1,), (1,)), ((), ()))

    def block(qh, bias, kj, vj, run, acc, masked):
        nz = lax.dot_general(qh, kj, nt, preferred_element_type=F32) + bias
        lk, lb = _log_gates(nz)
        if masked:
            lk = jnp.where(causal, lk, 0.0)
        after = jnp.dot(lk.astype(BF16), after_mat, preferred_element_type=F32)
        a = jnp.exp2(lb + after + run)
        if masked:
            a = jnp.where(causal, a, 0.0)
        acc = acc + jnp.dot(a.astype(BF16), vj, preferred_element_type=F32)
        run = run + jnp.sum(lk, axis=-1, keepdims=True)
        return run, acc

    group = min(ATTN_HEAD_GROUP, nheads)
    for g0 in range(0, nheads, group):
        heads = list(range(g0, g0 + group))
        cols = [slice(hh * HEAD_DIM, (hh + 1) * HEAD_DIM) for hh in heads]
        qs = [q_ref[:, c] for c in cols]
        biases = [bias_ref[0, hh] * (-LOG2E) for hh in heads]

        def blocks(s0, state, masked):
            out = []
            for n, c in enumerate(cols):
                run, acc = block(qs[n], biases[n], k_ref[pl.ds(s0, bk), c], v_ref[pl.ds(s0, bk), c],
                                 state[2 * n], state[2 * n + 1], masked)
                out += [run, acc]
            return tuple(out)

        init = tuple(jnp.zeros((bq, 1 if n % 2 == 0 else HEAD_DIM), F32) for n in range(2 * group))
        state = blocks(pl.multiple_of(qi * bk, bk), init, True)

        def body(step, carry):
            return blocks(pl.multiple_of((qi - 1 - step) * bk, bk), carry, False)

        state = lax.fori_loop(0, qi, body, state)
        for n, c in enumerate(cols):
            o_ref[:, c] = state[2 * n + 1].astype(o_ref.dtype)


def _attn_prompt(q, k, v, sb_bias2, layer, nbatch, seq):
    m, width = q.shape
    bq = min(ATTN_BQ, seq)
    nq = seq // bq
    return pl.pallas_call(
        _attn_prompt_kernel,
        grid=(nbatch, nq),
        in_specs=[
            pl.BlockSpec((1, sb_bias2.shape[1]), lambda b, i: (layer, 0), memory_space=pltpu.SMEM),
            pl.BlockSpec((bq, width), lambda b, i: (b * nq + i, 0)),
            pl.BlockSpec((seq, width), lambda b, i: (b, 0)),
            pl.BlockSpec((seq, width), lambda b, i: (b, 0)),
        ],
        out_specs=pl.BlockSpec((bq, width), lambda b, i: (b * nq + i, 0)),
        out_shape=jax.ShapeDtypeStruct((m, width), BF16),
        compiler_params=_cparams(("arbitrary", "arbitrary")),
        name="attn_prompt",
    )(sb_bias2, q, k, v)


def _attn_decode_kernel(npages, pt_ref, q_ref, bias_ref, *refs):
    del pt_ref
    k_refs = refs[:npages]
    v_refs = refs[npages:2 * npages]
    o_ref, run_ref, acc_ref = refs[2 * npages:]
    grp = pl.program_id(1)

    @pl.when(grp == 0)
    def _():
        run_ref[...] = jnp.zeros_like(run_ref)
        acc_ref[...] = jnp.zeros_like(acc_ref)

    q = q_ref[...]
    bias = bias_ref[...] * (-LOG2E)
    psize = k_refs[0].shape[0]
    run = run_ref[...]
    acc = acc_ref[...]
    for j in reversed(range(npages)):
        kj = k_refs[j][...]
        nz = jnp.sum(kj * q[None], axis=-1, keepdims=True) + bias[None]
        lk, lb = _log_gates(nz)
        tails = [None] * psize
        tail = run
        for s in reversed(range(psize)):
            tails[s] = tail
            tail = tail + lk[s]
        a = jnp.exp2(lb + jnp.stack(tails, axis=0))
        acc = acc + jnp.sum(a * v_refs[j][...], axis=0)
        run = tail
    run_ref[...] = run
    acc_ref[...] = acc

    @pl.when(grp == pl.num_programs(1) - 1)
    def _():
        o_ref[...] = acc


def _attn_decode(q3, sb_bias_col, cache_k, cache_v, page_table, layer):
    nseq, nheads, hd = q3.shape
    npages = page_table.shape[1]
    psize = cache_k.shape[2]
    per_step = min(DECODE_PAGES_PER_STEP, npages)
    ngrp = npages // per_step
    page_specs = []
    for j in range(per_step):
        page_specs.append(pl.BlockSpec(
            (None, None, psize, nheads, hd),
            lambda b, g, pt, j=j: (layer, pt[b, (ngrp - 1 - g) * per_step + j], 0, 0, 0)))
    grid_spec = pltpu.PrefetchScalarGridSpec(
        num_scalar_prefetch=1,
        grid=(nseq, ngrp),
        in_specs=[
            pl.BlockSpec((None, nheads, hd), lambda b, g, pt: (b, 0, 0)),
            pl.BlockSpec((None, nheads, 1), lambda b, g, pt: (layer, 0, 0)),
        ] + page_specs + page_specs,
        out_specs=pl.BlockSpec((None, nheads, hd), lambda b, g, pt: (b, 0, 0)),
        scratch_shapes=[pltpu.VMEM((nheads, 1), F32), pltpu.VMEM((nheads, hd), F32)],
    )
    return pl.pallas_call(
        functools.partial(_attn_decode_kernel, per_step),
        grid_spec=grid_spec,
        out_shape=jax.ShapeDtypeStruct((nseq, nheads, hd), F32),
        compiler_params=_cparams(("arbitrary", "arbitrary")),
        name="attn_decode",
    )(page_table, q3, sb_bias_col, *([cache_k] * per_step), *([cache_v] * per_step))


def _ln_swish(conv, bias, g, b):
    y = conv + bias
    mu = jnp.mean(y, axis=-1, keepdims=True)
    yc = y - mu
    var = jnp.mean(yc * yc, axis=-1, keepdims=True)
    return _silu(yc * lax.rsqrt(var + NORM_EPS) * g + b)


def _convb_prompt_kernel(u_ref, halo_ref, w_ref, bias_ref, g_ref, b_ref, y_ref, full_ref, shift_ref):
    tt = u_ref.shape[0]
    width = w_ref.shape[0]
    first = pl.program_id(1) == 0
    full_ref[0:CONV_HALO, :] = jnp.where(first, 0.0, halo_ref[...])
    full_ref[CONV_HALO:, :] = u_ref[...]
    nshift = tt + CONV_HALO - SUBLANES
    for r in range(1, SUBLANES):
        shift_ref[r, 0:nshift, :] = full_ref[pl.ds(r, nshift), :]
    base = CONV_HALO - (width - 1)
    acc = jnp.zeros(u_ref.shape, F32)
    for k in range(width):
        r = (base + k) % SUBLANES
        a = base + k - r
        rows = full_ref[pl.ds(a, tt), :] if r == 0 else shift_ref[r, pl.ds(a, tt), :]
        acc = acc + w_ref[k:k + 1, :] * rows
    y_ref[...] = _ln_swish(acc, bias_ref[...], g_ref[...], b_ref[...]).astype(y_ref.dtype)


def _convb_prompt(u, conv_w, conv_bias, ln_g, ln_b, layer, nbatch, seq):
    m, c = u.shape
    tt = min(CONV_TT, seq)
    nt = seq // tt
    width = conv_w.shape[1]
    per = tt // CONV_HALO
    vec = lambda b, t: (layer, 0, 0)
    return pl.pallas_call(
        _convb_prompt_kernel,
        grid=(nbatch, nt),
        in_specs=[
            pl.BlockSpec((tt, c), lambda b, t: (b * nt + t, 0)),
            pl.BlockSpec((CONV_HALO, c), lambda b, t: (jnp.maximum((b * nt + t) * per - 1, 0), 0)),
            pl.BlockSpec((None, width, c), vec),
            pl.BlockSpec((None, 1, c), vec),
            pl.BlockSpec((None, 1, c), vec),
            pl.BlockSpec((None, 1, c), vec),
        ],
        out_specs=pl.BlockSpec((tt, c), lambda b, t: (b * nt + t, 0)),
        out_shape=jax.ShapeDtypeStruct((m, c), BF16),
        scratch_shapes=[pltpu.VMEM((tt + CONV_HALO, c), F32), pltpu.VMEM((SUBLANES, tt + CONV_HALO, c), F32)],
        compiler_params=_cparams(("arbitrary", "arbitrary")),
        name="convb_prompt",
    )(u, u, conv_w, conv_bias, ln_g, ln_b)


def _convb_sample_kernel(hist_ref, u_ref, w_ref, bias_ref, g_ref, b_ref, y_ref, nh_ref):
    nhist = hist_ref.shape[0]
    hist = hist_ref[...]
    un = u_ref[...]
    conv = jnp.sum(w_ref[0:nhist, :] * hist, axis=0, keepdims=True) + w_ref[nhist:nhist + 1, :] * un
    y_ref[...] = _ln_swish(conv, bias_ref[...], g_ref[...], b_ref[...]).astype(y_ref.dtype)
    nh_ref[0:nhist - 1, :] = hist[1:, :]
    nh_ref[nhist - 1:nhist, :] = un


def _convb_sample(hist, u3, conv_w, conv_bias, ln_g, ln_b, layer):
    nseq, nhist, c = hist.shape[1:]
    width = conv_w.shape[1]
    vec = lambda b: (layer, 0, 0)
    return pl.pallas_call(
        _convb_sample_kernel,
        grid=(nseq,),
        in_specs=[
            pl.BlockSpec((None, None, nhist, c), lambda b: (layer, b, 0, 0)),
            pl.BlockSpec((None, 1, c), lambda b: (b, 0, 0)),
            pl.BlockSpec((None, width, c), vec),
            pl.BlockSpec((None, 1, c), vec),
            pl.BlockSpec((None, 1, c), vec),
            pl.BlockSpec((None, 1, c), vec),
        ],
        out_specs=[
            pl.BlockSpec((None, 1, c), lambda b: (b, 0, 0)),
            pl.BlockSpec((None, nhist, c), lambda b: (b, 0, 0)),
        ],
        out_shape=[
            jax.ShapeDtypeStruct((nseq, 1, c), BF16),
            jax.ShapeDtypeStruct((nseq, nhist, c), F32),
        ],
        compiler_params=_cparams(("arbitrary",)),
        name="convb_sample",
    )(hist, u3, conv_w, conv_bias, ln_g, ln_b)


def _shortconv_prompt_kernel(gb_ref, cv_ref, halo_ref, w_ref, o_ref, full_ref):
    tt = cv_ref.shape[0]
    width = w_ref.shape[0]
    first = pl.program_id(1) == 0
    full_ref[0:SHORT_HALO, :] = jnp.where(first, 0.0, halo_ref[...].astype(F32))
    full_ref[SHORT_HALO:, :] = cv_ref[...].astype(F32)
    base = SHORT_HALO - (width - 1)
    acc = jnp.zeros(cv_ref.shape, F32)
    for k in range(width):
        acc = acc + w_ref[k:k + 1, :] * full_ref[pl.ds(base + k, tt), :]
    o_ref[...] = (gb_ref[...].astype(F32) * acc).astype(o_ref.dtype)


def _shortconv_prompt(gate_b, cv, conv_w, layer, nbatch, seq):
    m, c = cv.shape
    tt = min(CONV_TT, seq)
    nt = seq // tt
    width = conv_w.shape[1]
    per = tt // SHORT_HALO
    return pl.pallas_call(
        _shortconv_prompt_kernel,
        grid=(nbatch, nt),
        in_specs=[
            pl.BlockSpec((tt, c), lambda b, t: (b * nt + t, 0)),
            pl.BlockSpec((tt, c), lambda b, t: (b * nt + t, 0)),
            pl.BlockSpec((SHORT_HALO, c), lambda b, t: (jnp.maximum((b * nt + t) * per - 1, 0), 0)),
            pl.BlockSpec((None, width, c), lambda b, t: (layer, 0, 0)),
        ],
        out_specs=pl.BlockSpec((tt, c), lambda b, t: (b * nt + t, 0)),
        out_shape=jax.ShapeDtypeStruct((m, c), BF16),
        scratch_shapes=[pltpu.VMEM((tt + SHORT_HALO, c), F32)],
        compiler_params=_cparams(("arbitrary", "arbitrary")),
        name="shortconv_prompt",
    )(gate_b, cv, cv, conv_w)


def _shortconv_sample_kernel(gb_ref, cv_ref, h0_ref, h1_ref, w_ref, o_ref):
    conv = w_ref[0:1, :] * h0_ref[...] + w_ref[1:2, :] * h1_ref[...] + w_ref[2:3, :] * cv_ref[...]
    o_ref[...] = (gb_ref[...].astype(F32) * conv).astype(o_ref.dtype)


def _shortconv_sample(gate_b, cv, h0, h1, conv_w, layer):
    m, c = cv.shape
    width = conv_w.shape[1]
    full = pl.BlockSpec((m, c), lambda i: (0, 0))
    return pl.pallas_call(
        _shortconv_sample_kernel,
        grid=(1,),
        in_specs=[full, full, full, full, pl.BlockSpec((None, width, c), lambda i: (layer, 0, 0))],
        out_specs=full,
        out_shape=jax.ShapeDtypeStruct((m, c), BF16),
        compiler_params=_cparams(("arbitrary",)),
        name="shortconv_sample",
    )(gate_b, cv, h0, h1, conv_w)


def _rank_kernel(idx_ref, start_ref, dest_ref, run_ref):
    tm = idx_ref.shape[0]

    @pl.when(pl.program_id(0) == 0)
    def _():
        run_ref[...] = start_ref[...]

    idxf = idx_ref[...].astype(F32)
    lane = lax.broadcasted_iota(I32, (tm, LANES), 1).astype(F32)
    onehots = [jnp.where(lane == idxf[:, k:k + 1], 1.0, 0.0) for k in range(TOP_K)]
    cnt = onehots[0]
    for oh in onehots[1:]:
        cnt = cnt + oh
    row = lax.broadcasted_iota(I32, (tm, tm), 0)
    col = lax.broadcasted_iota(I32, (tm, tm), 1)
    earlier = jnp.where(col < row, 1.0, 0.0).astype(BF16)
    before = jnp.dot(earlier, cnt.astype(BF16), preferred_element_type=F32)
    tot = run_ref[...] + before
    dests = [jnp.sum(oh * tot, axis=-1, keepdims=True) for oh in onehots]
    dest_ref[...] = _lane_place(dests, (tm, LANES), F32).astype(I32)
    run_ref[...] += jnp.sum(cnt, axis=0, keepdims=True)


def _rank(idx_all, start_rows):
    m = idx_all.shape[0]
    tm = ROW_TILE
    return pl.pallas_call(
        _rank_kernel,
        grid=(m // tm,),
        in_specs=[pl.BlockSpec((tm, LANES), lambda i: (i, 0)), pl.BlockSpec((1, LANES), lambda i: (0, 0))],
        out_specs=pl.BlockSpec((tm, LANES), lambda i: (i, 0)),
        out_shape=jax.ShapeDtypeStruct((m, LANES), I32),
        scratch_shapes=[pltpu.VMEM((1, LANES), F32)],
        compiler_params=_cparams(("arbitrary",)),
        name="moe_rank",
    )(idx_all, start_rows)


def _dispatch_kernel(ntile_a, pad_lo_ref, pad_hi_ref, dest_ref, ha_ref, hb_ref, xs_ref, zero_ref, sem, pad_sem):
    i = pl.program_id(0)
    tm = ha_ref.shape[0]
    pad_bits = MOE_BLOCK.bit_length() - 1

    def pad_copies(e, wait):
        lo = pad_lo_ref[e]
        n = pad_hi_ref[e] - lo
        off = lo
        for bit in reversed(range(pad_bits)):
            size = 1 << bit
            take = (n & size) != 0

            @pl.when(take)
            def _(off=off, size=size):
                cp = pltpu.make_async_copy(zero_ref.at[pl.ds(0, size)], xs_ref.at[pl.ds(off, size)], pad_sem)
                if wait:
                    cp.wait()
                else:
                    cp.start()

            off = off + jnp.where(take, size, 0)

    zrows = zero_ref.shape[0]
    tail_lo = pad_hi_ref[N_EXPERTS - 1]
    ntail = (xs_ref.shape[0] - tail_lo) // zrows

    def tail_copy(p):
        rows = pl.ds(pl.multiple_of(tail_lo + p * zrows, zrows), zrows)
        return pltpu.make_async_copy(zero_ref, xs_ref.at[rows], pad_sem)

    @pl.when(i == 0)
    def _():
        zero_ref[...] = jnp.zeros_like(zero_ref)

        def start_e(e, carry):
            pad_copies(e, False)
            return carry

        lax.fori_loop(0, N_EXPERTS, start_e, 0)

        def start_tail(p, carry):
            tail_copy(p).start()
            return carry

        lax.fori_loop(0, ntail, start_tail, 0)

    def scatter(h_ref):
        def issue(t, carry):
            for k in range(TOP_K):
                d = dest_ref[0, 0, t * TOP_K + k]
                pltpu.make_async_copy(h_ref.at[t], xs_ref.at[d], sem).start(priority=k % 2)
            return carry

        lax.fori_loop(0, tm, issue, 0)
        for _ in range(TOP_K):
            pltpu.make_async_copy(h_ref, xs_ref.at[pl.ds(0, tm)], sem).wait()

    @pl.when(i < ntile_a)
    def _():
        scatter(ha_ref)

    @pl.when(i >= ntile_a)
    def _():
        scatter(hb_ref)

    @pl.when(i == 0)
    def _():
        def wait_e(e, carry):
            pad_copies(e, True)
            return carry

        lax.fori_loop(0, N_EXPERTS, wait_e, 0)

        def wait_tail(p, carry):
            tail_copy(p).wait()
            return carry

        lax.fori_loop(0, ntail, wait_tail, 0)


def _dispatch(pad_lo, pad_hi, dest3, hrows_a, hrows_b, nrows):
    ntile = dest3.shape[0]
    tm = ROW_TILE
    nchunk = hrows_a.shape[1]
    ntile_a = hrows_a.shape[0] // tm
    ntile_b = hrows_b.shape[0] // tm
    assert ntile == ntile_a + ntile_b
    grid_spec = pltpu.PrefetchScalarGridSpec(
        num_scalar_prefetch=2,
        grid=(ntile,),
        in_specs=[
            pl.BlockSpec((1, 1, tm * TOP_K), lambda i, lo, hi: (i, 0, 0), memory_space=pltpu.SMEM),
            pl.BlockSpec((tm, nchunk, LANES), lambda i, lo, hi: (jnp.minimum(i, ntile_a - 1), 0, 0)),
            pl.BlockSpec((tm, nchunk, LANES), lambda i, lo, hi: (jnp.maximum(i - ntile_a, 0), 0, 0)),
        ],
        out_specs=pl.BlockSpec(memory_space=pl.ANY),
        scratch_shapes=[pltpu.VMEM((MOE_BLOCK // 2, nchunk, LANES), F32),
                        pltpu.SemaphoreType.DMA, pltpu.SemaphoreType.DMA],
    )
    return pl.pallas_call(
        functools.partial(_dispatch_kernel, ntile_a),
        grid_spec=grid_spec,
        out_shape=jax.ShapeDtypeStruct((nrows, nchunk, LANES), F32),
        compiler_params=_cparams(("arbitrary",)),
        name="moe_dispatch",
    )(pad_lo, pad_hi, dest3, hrows_a, hrows_b)


def _clamped_swiglu(gate, up):
    gate = jnp.minimum(gate, SWIGLU_LIMIT)
    up = jnp.clip(up, -SWIGLU_LIMIT, SWIGLU_LIMIT)
    return (up + 1.0) * gate * jax.nn.sigmoid(SWIGLU_ALPHA * gate)


def _experts_kernel(run_e_ref, run_b0_ref, run_nb_ref, nrun_ref, xs_ref, wg_ref, wu_ref, bg_ref, bu_ref,
                    wd_ref, bd_ref, ys_ref, xin_ref, xb_ref, acc_ref, wgb_ref, wub_ref, wdb_ref,
                    in_sem, out_sem):
    del run_e_ref
    r = pl.program_id(0)
    f = pl.program_id(1)
    nf = pl.num_programs(1)
    nrun = nrun_ref[0]
    nchunk, bm = xin_ref.shape[1], xin_ref.shape[2]
    rmax = xb_ref.shape[0] // bm
    nblk_total = ys_ref.shape[0] // bm

    def start_in(blk, slot):
        rows = pl.ds(pl.multiple_of(blk * bm, bm), bm)
        for c in range(nchunk):
            pltpu.make_async_copy(xs_ref.at[rows, c, :], xin_ref.at[slot, c], in_sem.at[slot]).start()

    def wait_in(slot):
        pltpu.make_async_copy(xin_ref.at[slot], xin_ref.at[slot], in_sem.at[slot]).wait()

    def start_out(j, blk):
        rows = pl.ds(pl.multiple_of(blk * bm, bm), bm)
        for c in range(nchunk):
            pltpu.make_async_copy(acc_ref.at[pl.ds(j * bm, bm), pl.ds(c * LANES, LANES)],
                                  ys_ref.at[rows, c, :], out_sem).start()

    def wait_out():
        pltpu.make_async_copy(acc_ref.at[pl.ds(0, bm)], acc_ref.at[pl.ds(0, bm)], out_sem).wait()

    def wait_out_n(n):
        def one(_, carry):
            wait_out()
            return carry

        lax.fori_loop(0, n, one, 0)

    @pl.when(r < nrun)
    def _():
        b0 = run_b0_ref[r]
        nb = run_nb_ref[r]

        @pl.when(f == 0)
        def _():
            @pl.when(r == 0)
            def _():
                start_in(b0, 0)
            for j in range(rmax):
                @pl.when(j < nb)
                def _(j=j):
                    if j + 1 < rmax:
                        @pl.when(j + 1 < nb)
                        def _():
                            start_in(b0 + (j + 1), (j + 1) % 2)
                    wait_in(j % 2)
                    xb_ref[j * bm:(j + 1) * bm, :] = jnp.concatenate(
                        [xin_ref[j % 2, c] for c in range(nchunk)], axis=-1).astype(BF16)

            @pl.when(r > 0)
            def _():
                wait_out_n(run_nb_ref[r - 1])

            for j in range(rmax):
                @pl.when(j < nb)
                def _(j=j):
                    acc_ref[j * bm:(j + 1) * bm, :] = jnp.broadcast_to(bd_ref[...], (bm, acc_ref.shape[1]))

        @pl.when((f == 1) & (r + 1 < nrun))
        def _():
            start_in(run_b0_ref[r + 1], 0)

        for j in range(rmax):
            @pl.when(j < nb)
            def _(j=j):
                rows = slice(j * bm, (j + 1) * bm)
                if j == 0:
                    wgb, wub, wdb = (w[...].astype(BF16) for w in (wg_ref, wu_ref, wd_ref))
                    wgb_ref[...] = wgb
                    wub_ref[...] = wub
                    wdb_ref[...] = wdb
                else:
                    wgb, wub, wdb = wgb_ref[...], wub_ref[...], wdb_ref[...]
                xb = xb_ref[rows, :]
                gate = jnp.dot(xb, wgb, preferred_element_type=F32) + bg_ref[...]
                up = jnp.dot(xb, wub, preferred_element_type=F32) + bu_ref[...]
                act = _clamped_swiglu(gate, up).astype(BF16)
                acc_ref[rows, :] += jnp.dot(act, wdb, preferred_element_type=F32)

                @pl.when(f == nf - 1)
                def _():
                    start_out(j, b0 + j)

        @pl.when((f == nf - 1) & (r == nrun - 1))
        def _():
            wait_out_n(nb)
            nused = b0 + nb

            @pl.when(nused < nblk_total)
            def _():
                acc_ref[0:bm, :] = jnp.zeros((bm, acc_ref.shape[1]), F32)

                def tail(p, carry):
                    start_out(0, p)
                    wait_out()
                    return carry

                lax.fori_loop(nused, nblk_total, tail, 0)


def _experts(run_e, run_b0, run_nb, nrun, xs, w_gu, b_gu4, w_dn, b_dn4, layer):
    nrows, nchunk, _ = xs.shape
    d = nchunk * LANES
    bm = MOE_BLOCK
    tf = MOE_FF_TILE
    dff = w_dn.shape[2]
    nf = dff // tf
    assert nf >= 2
    nrun_max = run_e.shape[0]
    rows_run = MOE_RUN_BLOCKS * bm

    def wmap(col):
        def index(r, f, re, rb, rn, nr):
            live = r < nr[0]
            e = re[jnp.minimum(r, nr[0] - 1)]
            return (layer, e) + col(jnp.where(live, f, nf - 1))
        return index

    grid_spec = pltpu.PrefetchScalarGridSpec(
        num_scalar_prefetch=4,
        grid=(nrun_max, nf),
        in_specs=[
            pl.BlockSpec(memory_space=pl.ANY),
            pl.BlockSpec((None, None, d, tf), wmap(lambda f: (0, f))),
            pl.BlockSpec((None, None, d, tf), wmap(lambda f: (0, nf + f))),
            pl.BlockSpec((None, None, 1, tf), wmap(lambda f: (0, f))),
            pl.BlockSpec((None, None, 1, tf), wmap(lambda f: (0, nf + f))),
            pl.BlockSpec((None, None, tf, d), wmap(lambda f: (f, 0))),
            pl.BlockSpec((None, None, 1, d), wmap(lambda f: (0, 0))),
        ],
        out_specs=pl.BlockSpec(memory_space=pl.ANY),
        scratch_shapes=[pltpu.VMEM((2, nchunk, bm, LANES), F32), pltpu.VMEM((rows_run, d), BF16),
                        pltpu.VMEM((rows_run, d), F32),
                        pltpu.VMEM((d, tf), BF16), pltpu.VMEM((d, tf), BF16), pltpu.VMEM((tf, d), BF16),
                        pltpu.SemaphoreType.DMA((2,)), pltpu.SemaphoreType.DMA],
    )
    return pl.pallas_call(
        _experts_kernel,
        grid_spec=grid_spec,
        out_shape=jax.ShapeDtypeStruct(xs.shape, F32),
        compiler_params=_cparams(("arbitrary", "arbitrary")),
        name="moe_experts",
    )(run_e, run_b0, run_nb, nrun, xs, w_gu, w_gu, b_gu4, b_gu4, w_dn, b_dn4)


def _combine_kernel(dest_ref, next_ref, x_ref, gate_ref, g_ref, ys_ref, o_ref, buf_ref, sem):
    i = pl.program_id(0)
    ntile = pl.num_programs(0)
    tm = x_ref.shape[0]
    nchunk = buf_ref.shape[2]

    def fetch(idx_ref, slot):
        def issue(t, carry):
            for k in range(TOP_K):
                d = idx_ref[0, 0, t * TOP_K + k]
                pltpu.make_async_copy(ys_ref.at[d], buf_ref.at[slot, k, :, t, :],
                                      sem.at[slot]).start(priority=k % 2)
            return carry

        lax.fori_loop(0, tm, issue, 0)

    @pl.when(i == 0)
    def _():
        fetch(dest_ref, 0)

    @pl.when(i + 1 < ntile)
    def _():
        fetch(next_ref, lax.rem(i + 1, 2))

    slot = lax.rem(i, 2)
    pltpu.make_async_copy(buf_ref.at[slot], buf_ref.at[slot], sem.at[slot]).wait()
    g = g_ref[...]
    for c in range(nchunk):
        cols = slice(c * LANES, (c + 1) * LANES)
        mix = None
        for k in range(TOP_K):
            part = g[:, k:k + 1] * buf_ref[slot, k, c]
            mix = part if mix is None else mix + part
        o_ref[:, cols] = x_ref[:, cols] + gate_ref[:, cols] * mix


def _combine(dest3, x, mod, slot, gates, ys):
    m, d = x.shape
    tm = ROW_TILE
    ntile = m // tm
    nchunk = ys.shape[1]
    return pl.pallas_call(
        _combine_kernel,
        grid=(ntile,),
        in_specs=[
            pl.BlockSpec((1, 1, tm * TOP_K), lambda i: (i, 0, 0), memory_space=pltpu.SMEM),
            pl.BlockSpec((1, 1, tm * TOP_K), lambda i: (jnp.minimum(i + 1, ntile - 1), 0, 0),
                         memory_space=pltpu.SMEM),
            pl.BlockSpec((tm, d), lambda i: (i, 0)),
            mod.spec(slot, 2, tm, d, 0),
            pl.BlockSpec((tm, LANES), lambda i: (i, 0)),
            pl.BlockSpec(memory_space=pl.ANY),
        ],
        out_specs=pl.BlockSpec((tm, d), lambda i: (i, 0)),
        out_shape=jax.ShapeDtypeStruct((m, d), F32),
        scratch_shapes=[pltpu.VMEM((2, TOP_K, nchunk, tm, LANES), F32), pltpu.SemaphoreType.DMA((2,))],
        compiler_params=_cparams(("arbitrary",)),
        name="moe_combine",
    )(dest3, dest3, x, mod.arr, gates, ys)


def _moe(xs_groups, mods, g4, slot, layer, w_router, b_router3, w_gu, b_gu4, w_dn, b_dn4):
    d = xs_groups[0].shape[1]
    nchunk = d // LANES
    tm = ROW_TILE
    routed = [_norm_router(x, g4, mod, slot, w_router, b_router3, layer, min(ROUTER_TM, x.shape[0]))
              for x, mod in zip(xs_groups, mods)]
    counts = sum(r[3] for r in routed)[0, :N_EXPERTS].astype(I32)
    npairs = sum(x.shape[0] for x in xs_groups) * TOP_K
    nblk = (npairs + N_EXPERTS * (MOE_BLOCK - 1)) // MOE_BLOCK
    blocks = (counts + MOE_BLOCK - 1) // MOE_BLOCK
    ends = jnp.cumsum(blocks)
    starts = ends - blocks
    start_rows = jnp.zeros((1, LANES), F32).at[0, :N_EXPERTS].set((starts * MOE_BLOCK).astype(F32))
    pad_lo = starts * MOE_BLOCK + counts
    pad_hi = ends * MOE_BLOCK
    runs = (blocks + MOE_RUN_BLOCKS - 1) // MOE_RUN_BLOCKS
    run_ends = jnp.cumsum(runs)
    run_starts = run_ends - runs
    nrun_max = (nblk + N_EXPERTS * (MOE_RUN_BLOCKS - 1)) // MOE_RUN_BLOCKS
    rid = jnp.arange(nrun_max, dtype=I32)
    run_e = jnp.minimum(jnp.sum((run_ends[None, :] <= rid[:, None]).astype(I32), axis=1), N_EXPERTS - 1)
    onehot_e = (run_e[:, None] == jnp.arange(N_EXPERTS, dtype=I32)[None, :]).astype(I32)
    local = rid - jnp.sum(onehot_e * run_starts[None, :], axis=1)
    run_b0 = jnp.sum(onehot_e * starts[None, :], axis=1) + local * MOE_RUN_BLOCKS
    run_nb = jnp.clip(jnp.sum(onehot_e * blocks[None, :], axis=1) - local * MOE_RUN_BLOCKS, 0, MOE_RUN_BLOCKS)
    nrun = run_ends[-1:].astype(I32)

    idx_all = jnp.concatenate([r[1] for r in routed], axis=0)
    dest_all = _rank(idx_all, start_rows)
    mtot = idx_all.shape[0]
    dest3 = dest_all[:, :TOP_K].reshape(mtot // tm, 1, tm * TOP_K)
    hrows = [r[0].reshape(-1, nchunk, LANES) for r in routed]
    xs = _dispatch(pad_lo, pad_hi, dest3, hrows[0], hrows[1], nblk * MOE_BLOCK)
    ys = _experts(run_e, run_b0.astype(I32), run_nb.astype(I32), nrun, xs, w_gu, b_gu4, w_dn, b_dn4, layer)
    outs, off = [], 0
    for x, mod, r in zip(xs_groups, mods, routed):
        nt = x.shape[0] // tm
        outs.append(_combine(dest3[off:off + nt], x, mod, slot, r[2], ys))
        off += nt
    return outs


def kernel(x_prompt, x_sample, c_prompt, c_sample, cache_k, cache_v, state_conv_b, state_conv_c, page_table,
           norm_g, w_ada, b_ada, w_in_a, g_q, g_k, sb_bias, conv_b_w, conv_b_bias, ln_b_g, ln_b_b, w_out_a,
           w_in_c, conv_c_w, w_out_c, w_router, b_router, w_gate_up, b_gate_up, w_down, b_down):
    nbatch, seq, d = x_prompt.shape
    nsamp = x_sample.shape[0]
    depth = norm_g.shape[0]
    a_width = g_q.shape[-1] * sb_bias.shape[-1]
    nheads = sb_bias.shape[-1]
    b_width = conv_b_w.shape[-1]
    c_width = conv_c_w.shape[-1]

    xp = x_prompt.reshape(nbatch * seq, d)
    xs_ = x_sample.reshape(nsamp, d)

    mod_all = _ada_all(jnp.concatenate([c_prompt, c_sample], axis=0), w_ada, b_ada)
    mod_p = _Mod(mod_all[:, :nbatch].reshape(mod_all.shape[0], nbatch, 1, 3 * d), False, seq, d)
    mod_s = _Mod(mod_all[:, nbatch:], True, 1, d)
    g4 = norm_g.reshape(-1, 1, d)

    g_q3 = g_q.reshape(-1, 1, HEAD_DIM)
    g_k3 = g_k.reshape(-1, 1, HEAD_DIM)
    sb_col = sb_bias.reshape(-1, nheads, 1)
    conv_b_bias3 = conv_b_bias.reshape(-1, 1, b_width)
    ln_g3 = ln_b_g.reshape(-1, 1, b_width)
    ln_b3 = ln_b_b.reshape(-1, 1, b_width)
    b_router3 = b_router.reshape(depth, 1, -1)
    b_gu4 = b_gate_up.reshape(depth, N_EXPERTS, 1, -1)
    b_dn4 = b_down.reshape(depth, N_EXPERTS, 1, -1)

    new_k, new_v, new_b, new_c = [[], []], [[], []], [[], []], [[], []]
    for i in range(depth):
        j = i // 2
        slot = 2 * i
        hp = _norm_mod(xp, g4, mod_p, slot, min(PROJ_TM, xp.shape[0]))
        hs = _norm_mod(xs_, g4, mod_s, slot, nsamp)
        if i % 2 == 0:
            outs = []
            for grp, h in enumerate((hp, hs)):
                (q,) = _proj(h, w_in_a, j, [0], a_width, "q", [BF16 if grp == 0 else F32], g_q3)
                k, kb = _proj(h, w_in_a, j, [a_width], a_width, "k", [F32, BF16], g_k3)
                v, vb = _proj(h, w_in_a, j, [2 * a_width], a_width, "v", [F32, BF16])
                (u,) = _proj(h, w_in_a, j, [3 * a_width, 3 * a_width + b_width], b_width, "glu", [F32])
                new_k[grp].append(k)
                new_v[grp].append(v)
                outs.append((q, kb, vb, u))
            q, kb, vb, u = outs[0]
            o_a = _attn_prompt(q, kb, vb, sb_bias, j, nbatch, seq)
            y_b = _convb_prompt(u, conv_b_w, conv_b_bias3, ln_g3, ln_b3, j, nbatch, seq)
            new_b[0].append(u.reshape(nbatch, seq, b_width)[:, seq - (conv_b_w.shape[1] - 1):])
            xp = _out_proj([o_a, y_b], w_out_a, j, xp, mod_p, slot)

            q, _, _, u = outs[1]
            o_s = _attn_decode(q.reshape(nsamp, nheads, HEAD_DIM), sb_col, cache_k, cache_v, page_table, j)
            y_s, hist_b = _convb_sample(state_conv_b, u.reshape(nsamp, 1, b_width), conv_b_w, conv_b_bias3,
                                        ln_g3, ln_b3, j)
            new_b[1].append(hist_b)
            xs_ = _out_proj([o_s.reshape(nsamp, a_width).astype(BF16), y_s.reshape(nsamp, b_width)],
                            w_out_a, j, xs_, mod_s, slot)
        else:
            (gb_p,) = _proj(hp, w_in_c, j, [0], c_width, "id", [BF16])
            (cv_p,) = _proj(hp, w_in_c, j, [c_width, 2 * c_width], c_width, "mul", [BF16])
            lhs_p = _shortconv_prompt(gb_p, cv_p, conv_c_w, j, nbatch, seq)
            nh = conv_c_w.shape[1] - 1
            new_c[0].append(cv_p.reshape(nbatch, seq, c_width)[:, seq - nh:].astype(F32))
            xp = _out_proj([lhs_p], w_out_c, j, xp, mod_p, slot)

            (gb_s,) = _proj(hs, w_in_c, j, [0], c_width, "id", [BF16])
            (cv_s,) = _proj(hs, w_in_c, j, [c_width, 2 * c_width], c_width, "mul", [F32])
            h0 = state_conv_c[j, :, 0, :]
            h1 = state_conv_c[j, :, 1, :]
            lhs_s = _shortconv_sample(gb_s, cv_s, h0, h1, conv_c_w, j)
            new_c[1].append(jnp.stack([h1, cv_s], axis=1))
            xs_ = _out_proj([lhs_s], w_out_c, j, xs_, mod_s, slot)

        xp, xs_ = _moe([xp, xs_], [mod_p, mod_s], g4, slot + 1, i, w_router, b_router3,
                       w_gate_up, b_gu4, w_down, b_dn4)

    def heads(a, n, t):
        return a.reshape(n, t, nheads, HEAD_DIM)

    return (
        xp.reshape(nbatch, seq, d),
        xs_.reshape(nsamp, 1, d),
        jnp.stack([heads(k, nbatch, seq) for k in new_k[0]]),
        jnp.stack([heads(v, nbatch, seq) for v in new_v[0]]),
        jnp.stack(new_b[0]),
        jnp.stack(new_c[0]),
        jnp.stack([heads(k, nsamp, 1) for k in new_k[1]]),
        jnp.stack([heads(v, nsamp, 1) for v in new_v[1]]),
        jnp.stack(new_b[1]),
        jnp.stack(new_c[1]),
    )
```
